```python
import jax, jax.numpy as jnp
from jax import lax
import numpy as np

D_MODEL = 1024
BATCH = 2
SEQ = 8192
DEPTH = 4
DEC_BATCH = 128
DEC_SEQ = 1
PAST_LEN = 8192
PAGE_SIZE = 128

N_EVEN = (DEPTH + 1) // 2
N_ODD = DEPTH // 2
EPS = 1e-6

DN_HEADS = 4
DN_DK = 128
DN_DV = 128
DN_CONV = 4
DN_CHUNK = 64
SW_HEADS = 8
SW_KV_HEADS = 2
SW_GROUP = SW_HEADS // SW_KV_HEADS
SW_HD = 64
WINDOW = 128
GLA_HEADS = 4
GLA_DK = 128
GLA_DV = 256
GLA_RANK = 16
GLA_TAU = 16.0
GLA_CHUNK = 64
D_FF = 4 * D_MODEL

DN_QK = DN_HEADS * DN_DK
DN_V = DN_HEADS * DN_DV
DN_CONV_CH = 2 * DN_QK + DN_V
SW_Q = SW_HEADS * SW_HD
SW_KV = SW_KV_HEADS * SW_HD
EVEN_IN = DN_CONV_CH + DN_V + 2 * DN_HEADS + SW_Q + 2 * SW_KV
EVEN_MIX = DN_V + SW_Q
GLA_QK = GLA_HEADS * GLA_DK
GLA_V = GLA_HEADS * GLA_DV
ODD_IN = 2 * GLA_QK + 2 * GLA_V + GLA_RANK

kernel_name = "hybrid_deltanet_swa_gla_step"


def _split(a, sizes):
    out, o = [], 0
    for s in sizes:
        out.append(a[..., o:o + s])
        o += s
    return out


def _pad_time(a, lp):
    pad = lp - a.shape[1]
    if pad == 0:
        return a
    widths = [(0, 0)] * a.ndim
    widths[1] = (0, pad)
    return jnp.pad(a, widths)


def rmsnorm(x, g):
    xf = x.astype(jnp.float32)
    y = xf * lax.rsqrt(jnp.mean(xf * xf, -1, keepdims=True) + EPS)
    return (y * g.astype(jnp.float32)).astype(x.dtype)


def l2norm(x):
    xf = x.astype(jnp.float32)
    return xf * lax.rsqrt(jnp.sum(xf * xf, -1, keepdims=True) + EPS)


def causal_conv(u, buf, w):
    L = u.shape[1]
    ext = jnp.concatenate([buf.astype(u.dtype), u], 1)
    out = ext[:, 0:L] * w[0]
    for i in range(1, DN_CONV):
        out = out + ext[:, i:i + L] * w[i]
    return out, ext[:, ext.shape[1] - (DN_CONV - 1):]


def _to_chunks(a, C, nc):
    B = a.shape[0]
    a = _pad_time(a.astype(jnp.float32), nc * C)
    a = a.reshape((B, nc, C) + a.shape[2:])
    a = jnp.moveaxis(a, 3, 2)
    return jnp.moveaxis(a, 1, 0)


def _from_chunks(o, L):
    nc, B, H, C, d = o.shape
    o = jnp.swapaxes(jnp.moveaxis(o, 0, 1), 2, 3).reshape(B, nc * C, H, d)
    return o[:, :L]


def gated_delta_chunked(q, k, v, g, beta, S0):
    L = q.shape[1]
    C = min(DN_CHUNK, L)
    nc = -(-L // C)
    qc, kc, vc, gc, bc = (_to_chunks(t, C, nc) for t in (q, k, v, g, beta))
    tri_incl = jnp.tril(jnp.ones((C, C), bool))
    tri_strict = jnp.tril(jnp.ones((C, C), jnp.float32), -1)
    eye = jnp.eye(C, dtype=jnp.float32)

    def step(S, inp):
        qi, ki, vi, gi, bi = inp
        G = jnp.cumsum(gi, -1)
        decay = jnp.exp(jnp.where(tri_incl, G[..., :, None] - G[..., None, :], -jnp.inf))
        kb = ki * bi[..., None]
        M = jnp.einsum('bhid,bhjd->bhij', kb, ki) * decay * tri_strict
        T = lax.linalg.triangular_solve(eye + M, jnp.broadcast_to(eye, M.shape),
                                        left_side=True, lower=True, unit_diagonal=True)
        eG = jnp.exp(G)[..., None]
        u = T @ (vi * bi[..., None])
        w = T @ (kb * eG)
        v_new = u - w @ S
        attn = jnp.einsum('bhid,bhjd->bhij', qi, ki) * decay
        o = (qi * eG) @ S + attn @ v_new
        g_last = G[..., -1]
        S = S * jnp.exp(g_last)[..., None, None] + jnp.einsum(
            'bhcd,bhce->bhde', ki * jnp.exp(g_last[..., None] - G)[..., None], v_new)
        return S, o

    S, o = lax.scan(step, S0, (qc, kc, vc, gc, bc))
    return _from_chunks(o, L), S


def gla_chunked(q, k, v, glog, S0):
    L = q.shape[1]
    C = min(GLA_CHUNK, L)
    nc = -(-L // C)
    qc, kc, vc, gc = (_to_chunks(t, C, nc) for t in (q, k, v, glog))
    tri_incl = jnp.tril(jnp.ones((C, C), bool))[:, :, None]

    def step(S, inp):
        qi, ki, vi, gi = inp
        G = jnp.cumsum(gi, axis=2)
        dec = jnp.exp(jnp.where(tri_incl, G[:, :, :, None, :] - G[:, :, None, :, :], -jnp.inf))
        attn = jnp.einsum('bhid,bhjd,bhijd->bhij', qi, ki, dec)
        o = (qi * jnp.exp(G)) @ S + attn @ vi
        gl = G[:, :, -1]
        S = S * jnp.exp(gl)[..., None] + jnp.einsum(
            'bhcd,bhce->bhde', ki * jnp.exp(gl[:, :, None] - G), vi)
        return S, o

    S, o = lax.scan(step, S0, (qc, kc, vc, gc))
    return _from_chunks(o, L), S


def swa_attention(q, k, v, k_past, v_past, sinks, start):
    B, L = q.shape[:2]
    Qb = min(WINDOW, L)
    nb = -(-L // Qb)
    Lp = nb * Qb
    k_all = jnp.concatenate([k_past.astype(k.dtype), _pad_time(k, Lp)], 1)
    v_all = jnp.concatenate([v_past.astype(v.dtype), _pad_time(v, Lp)], 1)
    idx = (jnp.arange(nb) * Qb)[:, None] + jnp.arange(WINDOW + Qb)[None]
    kb = k_all[:, idx]
    vb = v_all[:, idx]
    qb = _pad_time(q, Lp).reshape(B, nb, Qb, SW_KV_HEADS, SW_GROUP, SW_HD)
    qpos = start + jnp.arange(Lp).reshape(nb, Qb)
    kpos = start - WINDOW + idx
    rel = qpos[:, :, None] - kpos[:, None, :]
    valid = (rel >= 0) & (rel < WINDOW) & (kpos[:, None, :] >= 0)
    slopes = jnp.exp2(-8.0 * jnp.arange(1, SW_HEADS + 1, dtype=jnp.float32) / SW_HEADS)
    slopes = slopes.reshape(SW_KV_HEADS, SW_GROUP)[:, :, None, None]
    s = jnp.einsum('bnqkgd,bnskd->bnkgqs', qb, kb).astype(jnp.float32) * (SW_HD ** -0.5)
    s = s - slopes * rel[:, None, None].astype(jnp.float32)
    s = jnp.where(valid[None, :, None, None], s, -jnp.inf)
    sink = sinks.astype(jnp.float32).reshape(SW_KV_HEADS, SW_GROUP)[:, :, None, None]
    m = jnp.maximum(jnp.max(s, -1, keepdims=True), sink)
    p = jnp.exp(s - m)
    p = p / (jnp.sum(p, -1, keepdims=True) + jnp.exp(sink - m))
    o = jnp.einsum('bnkgqs,bnskd->bnqkgd', p.astype(v.dtype), vb)
    o = o.reshape(B, Lp, SW_Q)[:, :L]
    new_k = jnp.concatenate([k_past.astype(k.dtype), k], 1)[:, L:]
    new_v = jnp.concatenate([v_past.astype(v.dtype), v], 1)[:, L:]
    return o, new_k, new_v


def even_mixer(h, start, S0, conv_buf, k_past, v_past, w_in, conv_w, a_log, dt_bias, dn_norm, sinks, w_out):
    B, L, _ = h.shape
    f32 = jnp.float32
    proj = h @ w_in
    qkv, z, a, b, q_sw, k_sw, v_sw = _split(proj, [DN_CONV_CH, DN_V, DN_HEADS, DN_HEADS, SW_Q, SW_KV, SW_KV])
    conv, new_buf = causal_conv(qkv, conv_buf, conv_w)
    conv = jax.nn.silu(conv.astype(f32))
    qd, kd, vd = _split(conv, [DN_QK, DN_QK, DN_V])
    qd = l2norm(qd.reshape(B, L, DN_HEADS, DN_DK)) * (DN_DK ** -0.5)
    kd = l2norm(kd.reshape(B, L, DN_HEADS, DN_DK))
    vd = vd.reshape(B, L, DN_HEADS, DN_DV)
    beta = jax.nn.sigmoid(b.astype(f32))
    g = -jnp.exp(a_log.astype(f32)) * jax.nn.softplus(a.astype(f32) + dt_bias.astype(f32))
    o_dn, S = gated_delta_chunked(qd, kd, vd, g, beta, S0.astype(f32))
    o_dn = rmsnorm(o_dn, dn_norm) * jax.nn.silu(z.astype(f32)).reshape(B, L, DN_HEADS, DN_DV)
    o_dn = o_dn.reshape(B, L, DN_V).astype(h.dtype)
    o_sw, new_k, new_v = swa_attention(q_sw.reshape(B, L, SW_HEADS, SW_HD),
                                       k_sw.reshape(B, L, SW_KV_HEADS, SW_HD),
                                       v_sw.reshape(B, L, SW_KV_HEADS, SW_HD),
                                       k_past, v_past, sinks, start)
    out = jnp.concatenate([o_dn, o_sw.astype(h.dtype)], -1) @ w_out
    return out, S.astype(h.dtype), new_buf, new_k, new_v


def odd_mixer(h, S0, w_in, w_gate_up, b_gate, gla_norm, w_out):
    B, L, _ = h.shape
    f32 = jnp.float32
    proj = h @ w_in
    q, k, v, r, gdown = _split(proj, [GLA_QK, GLA_QK, GLA_V, GLA_V, GLA_RANK])
    glog = jax.nn.log_sigmoid((gdown @ w_gate_up + b_gate).astype(f32)) / GLA_TAU
    q = q.astype(f32).reshape(B, L, GLA_HEADS, GLA_DK) * (GLA_DK ** -0.5)
    k = k.astype(f32).reshape(B, L, GLA_HEADS, GLA_DK)
    v = v.astype(f32).reshape(B, L, GLA_HEADS, GLA_DV)
    o, S = gla_chunked(q, k, v, glog.reshape(B, L, GLA_HEADS, GLA_DK), S0.astype(f32))
    o = rmsnorm(o, gla_norm) * jax.nn.silu(r.astype(f32)).reshape(B, L, GLA_HEADS, GLA_DV)
    out = o.reshape(B, L, GLA_V).astype(h.dtype) @ w_out
    return out, S.astype(h.dtype)


def trunk(x, start, st_dn, st_conv, c_k, c_v, st_gla,
          norm_mix, norm_mlp, norm_final,
          even_w_in, dn_conv_w, dn_a_log, dn_dt_bias, dn_norm, sw_sinks, even_w_out,
          gla_w_in, gla_w_gate_up, gla_b_gate, gla_norm, gla_w_out,
          mlp_w_up, mlp_w_down):
    n_dn, n_conv, n_k, n_v, n_gla = [], [], [], [], []
    for layer in range(DEPTH):
        h = rmsnorm(x, norm_mix[layer])
        if layer % 2 == 0:
            e = layer // 2
            mix, S, buf, nk, nv = even_mixer(h, start, st_dn[e], st_conv[e], c_k[e], c_v[e],
                                             even_w_in[e], dn_conv_w[e], dn_a_log[e], dn_dt_bias[e],
                                             dn_norm[e], sw_sinks[e], even_w_out[e])
            n_dn.append(S); n_conv.append(buf); n_k.append(nk); n_v.append(nv)
        else:
            o_i = layer // 2
            mix, S = odd_mixer(h, st_gla[o_i], gla_w_in[o_i], gla_w_gate_up[o_i], gla_b_gate[o_i],
                               gla_norm[o_i], gla_w_out[o_i])
            n_gla.append(S)
        x = x + mix
        h = rmsnorm(x, norm_mlp[layer])
        x = x + jnp.square(jax.nn.relu(h @ mlp_w_up[layer])) @ mlp_w_down[layer]
    y = rmsnorm(x, norm_final)
    return y, jnp.stack(n_dn), jnp.stack(n_conv), jnp.stack(n_k), jnp.stack(n_v), jnp.stack(n_gla)


def setup_inputs(seed: int = 0) -> dict:
    key = jax.random.key(seed)
    ks = iter(jax.random.split(key, 32))
    nrm = lambda shape, s: jax.random.normal(next(ks), shape, jnp.float32) * s
    return {
        "x_prompt": nrm((BATCH, SEQ, D_MODEL), 1.0),
        "x_sample": nrm((DEC_BATCH, DEC_SEQ, D_MODEL), 1.0),
        "state_dn": nrm((N_EVEN, DEC_BATCH, DN_HEADS, DN_DK, DN_DV), 0.1),
        "state_dn_conv": nrm((N_EVEN, DEC_BATCH, DN_CONV - 1, DN_CONV_CH), 1.0),
        "cache_swa_k": nrm((N_EVEN, DEC_BATCH, WINDOW, SW_KV_HEADS, SW_HD), 1.0),
        "cache_swa_v": nrm((N_EVEN, DEC_BATCH, WINDOW, SW_KV_HEADS, SW_HD), 1.0),
        "state_gla": nrm((N_ODD, DEC_BATCH, GLA_HEADS, GLA_DK, GLA_DV), 0.3),
        "norm_mix": 1.0 + nrm((DEPTH, D_MODEL), 0.02),
        "norm_mlp": 1.0 + nrm((DEPTH, D_MODEL), 0.02),
        "norm_final": 1.0 + nrm((D_MODEL,), 0.02),
        "even_w_in": nrm((N_EVEN, D_MODEL, EVEN_IN), D_MODEL ** -0.5),
        "dn_conv_w": nrm((N_EVEN, DN_CONV, DN_CONV_CH), DN_CONV ** -0.5),
        "dn_a_log": jnp.log(jax.random.uniform(next(ks), (N_EVEN, DN_HEADS), jnp.float32, 1.0, 16.0)),
        "dn_dt_bias": nrm((N_EVEN, DN_HEADS), 0.1),
        "dn_norm": 1.0 + nrm((N_EVEN, DN_DV), 0.02),
        "sw_sinks": nrm((N_EVEN, SW_HEADS), 0.5),
        "even_w_out": nrm((N_EVEN, EVEN_MIX, D_MODEL), EVEN_MIX ** -0.5),
        "gla_w_in": nrm((N_ODD, D_MODEL, ODD_IN), D_MODEL ** -0.5),
        "gla_w_gate_up": nrm((N_ODD, GLA_RANK, GLA_QK), GLA_RANK ** -0.5),
        "gla_b_gate": nrm((N_ODD, GLA_QK), 0.1),
        "gla_norm": 1.0 + nrm((N_ODD, GLA_DV), 0.02),
        "gla_w_out": nrm((N_ODD, GLA_V, D_MODEL), GLA_V ** -0.5),
        "mlp_w_up": nrm((DEPTH, D_MODEL, D_FF), D_MODEL ** -0.5),
        "mlp_w_down": nrm((DEPTH, D_FF, D_MODEL), D_FF ** -0.5),
    }


def reference(x_prompt, x_sample, state_dn, state_dn_conv, cache_swa_k, cache_swa_v, state_gla,
              norm_mix, norm_mlp, norm_final,
              even_w_in, dn_conv_w, dn_a_log, dn_dt_bias, dn_norm, sw_sinks, even_w_out,
              gla_w_in, gla_w_gate_up, gla_b_gate, gla_norm, gla_w_out,
              mlp_w_up, mlp_w_down):
    dt = x_prompt.dtype
    weights = (norm_mix, norm_mlp, norm_final,
               even_w_in, dn_conv_w, dn_a_log, dn_dt_bias, dn_norm, sw_sinks, even_w_out,
               gla_w_in, gla_w_gate_up, gla_b_gate, gla_norm, gla_w_out,
               mlp_w_up, mlp_w_down)
    z_dn = jnp.zeros((N_EVEN, BATCH, DN_HEADS, DN_DK, DN_DV), dt)
    z_conv = jnp.zeros((N_EVEN, BATCH, DN_CONV - 1, DN_CONV_CH), dt)
    z_kv = jnp.zeros((N_EVEN, BATCH, WINDOW, SW_KV_HEADS, SW_HD), dt)
    z_gla = jnp.zeros((N_ODD, BATCH, GLA_HEADS, GLA_DK, GLA_DV), dt)
    y_prompt, p_dn, p_conv, p_k, p_v, p_gla = trunk(x_prompt, 0, z_dn, z_conv, z_kv, z_kv, z_gla, *weights)
    y_sample, s_dn, s_conv, s_k, s_v, s_gla = trunk(x_sample, PAST_LEN, state_dn, state_dn_conv,
                                                    cache_swa_k, cache_swa_v, state_gla, *weights)
    return (y_prompt, y_sample, p_dn, p_conv, p_k, p_v, p_gla, s_dn, s_conv, s_k, s_v, s_gla)
```

```python
import functools

import jax
import jax.numpy as jnp
from jax import lax
from jax.experimental import pallas as pl
from jax.experimental.pallas import tpu as pltpu

F32 = jnp.float32
BF16 = jnp.bfloat16
EPS = 1e-6

DN_HEADS = 4
DN_DK = 128
DN_DV = 128
DN_CONV = 4
SW_HEADS = 8
SW_KV_HEADS = 2
SW_GROUP = SW_HEADS // SW_KV_HEADS
SW_HD = 64
WINDOW = 128
GLA_HEADS = 4
GLA_DK = 128
GLA_DV = 256
GLA_RANK = 16
GLA_TAU = 16.0

LANES = 128
SUBLANES = 8
CHUNK = 64
SUB = 16
VMEM_LIMIT = 56 * 1024 * 1024


def _dot(a, b):
    return jnp.dot(a, b, preferred_element_type=F32)


def _dot_nt(a, b):
    return lax.dot_general(a, b, (((1,), (1,)), ((), ())), preferred_element_type=F32)


def _dot_tn(a, b):
    return lax.dot_general(a, b, (((0,), (0,)), ((), ())), preferred_element_type=F32)


def _dot_f32(a, b):
    return jnp.dot(a, b, preferred_element_type=F32, precision=lax.Precision.HIGHEST)


def _rms(x, g):
    return x * lax.rsqrt(jnp.mean(x * x, axis=-1, keepdims=True) + EPS) * g


def _silu(x):
    return x * jax.nn.sigmoid(x)


def _softplus(x):
    return jnp.maximum(x, 0.0) + jnp.log1p(jnp.exp(-jnp.abs(x)))


def _log_sigmoid(x):
    return jnp.minimum(x, 0.0) - jnp.log1p(jnp.exp(-jnp.abs(x)))


def _cparams(sem):
    return pltpu.CompilerParams(dimension_semantics=sem, vmem_limit_bytes=VMEM_LIMIT)


def _const_spec(shape):
    nd = len(shape)
    return pl.BlockSpec(shape, lambda *_: (0,) * nd)


EV_QKV = 3 * DN_HEADS * DN_DK
EV_Z = EV_QKV
EV_QSW = EV_Z + 512
EV_KSW = EV_QSW + 512
EV_VSW = EV_KSW + 128
EV_AB = EV_VSW + 128
EV_COLS = EV_AB + LANES


def _even_in_kernel(x_ref, nw_ref, w_ref, cw_ref, alog_ref, dtb_ref, hist_ref,
                    q_ref, k_ref, v_ref, z_ref, gb_ref, qsw_ref, ksw_ref, vsw_ref, upre_ref,
                    carry_ref, *, tiles_per_seq, per_token_hist):
    i = pl.program_id(0)
    tm = x_ref.shape[0]
    h = _rms(x_ref[...], nw_ref[...]).astype(BF16)

    if not per_token_hist:
        @pl.when(i % tiles_per_seq == 0)
        def _():
            carry_ref[...] = jnp.zeros_like(carry_ref)

    row8 = lax.broadcasted_iota(jnp.int32, (SUBLANES, 512), 0)

    def conv_section(u, c0):
        cw = cw_ref[:, c0:c0 + 512]
        if per_token_hist:
            out = u * cw[3:4]
            for t in range(DN_CONV - 1):
                out = out + hist_ref[t, :, c0:c0 + 512] * cw[t:t + 1]
            return out
        prev = carry_ref[:, c0:c0 + 512]
        out = u * cw[3:4]
        for s in range(1, DN_CONV):
            sh = pltpu.roll(u, s, 0)
            top = jnp.where(row8 < s, pltpu.roll(prev, s, 0), sh[0:SUBLANES])
            sh = jnp.concatenate([top, sh[SUBLANES:]], axis=0)
            out = out + sh * cw[3 - s:4 - s]
        carry_ref[:, c0:c0 + 512] = u[tm - SUBLANES:tm]
        return out

    def l2n(y):
        outs = []
        for hh in range(DN_HEADS):
            yh = y[:, hh * DN_DK:(hh + 1) * DN_DK]
            outs.append(yh * lax.rsqrt(jnp.sum(yh * yh, axis=-1, keepdims=True) + EPS))
        return jnp.concatenate(outs, axis=1)

    uq = _dot(h, w_ref[:, 0:512])
    upre_ref[:, 0:512] = uq
    q_ref[...] = l2n(_silu(conv_section(uq, 0))) * (DN_DK ** -0.5)
    uk = _dot(h, w_ref[:, 512:1024])
    upre_ref[:, 512:1024] = uk
    k_ref[...] = l2n(_silu(conv_section(uk, 512)))
    uv = _dot(h, w_ref[:, 1024:1536])
    upre_ref[:, 1024:1536] = uv
    v_ref[...] = _silu(conv_section(uv, 1024))
    z_ref[...] = _dot(h, w_ref[:, EV_Z:EV_Z + 512])
    qsw_ref[...] = _dot(h, w_ref[:, EV_QSW:EV_QSW + 512])
    ksw_ref[...] = _dot(h, w_ref[:, EV_KSW:EV_KSW + 128])
    vsw_ref[...] = _dot(h, w_ref[:, EV_VSW:EV_VSW + 128])
    ab = _dot(h, w_ref[:, EV_AB:EV_AB + LANES])
    g = -jnp.exp(alog_ref[...]) * _softplus(ab + dtb_ref[...])
    beta = jax.nn.sigmoid(ab)
    lane = lax.broadcasted_iota(jnp.int32, ab.shape, 1)
    gb = jnp.where(lane < DN_HEADS, g, beta)
    gb_ref[...] = gb[:, 0:2 * DN_HEADS]


def _even_in(x, nw, w, cw, alog_row, dtb_row, hist, *, tm, seq_len, per_token_hist):
    n, d = x.shape
    grid = (n // tm,)
    tiles_per_seq = max(seq_len // tm, 1)
    row = lambda c: pl.BlockSpec((tm, c), lambda i: (i, 0))
    if per_token_hist:
        hist_spec = pl.BlockSpec((DN_CONV - 1, tm, EV_QKV), lambda i: (0, i, 0))
    else:
        hist_spec = _const_spec(hist.shape)
    out_shapes = [jax.ShapeDtypeStruct((n, c), F32) for c in (512, 512, 512, 512, 2 * DN_HEADS, 512, 128, 128, EV_QKV)]
    out_specs = [row(c) for c in (512, 512, 512, 512, 2 * DN_HEADS, 512, 128, 128, EV_QKV)]
    return pl.pallas_call(
        functools.partial(_even_in_kernel, tiles_per_seq=tiles_per_seq, per_token_hist=per_token_hist),
        grid=grid,
        in_specs=[row(d), _const_spec(nw.shape), _const_spec(w.shape), _const_spec(cw.shape),
                  _const_spec(alog_row.shape), _const_spec(dtb_row.shape), hist_spec],
        out_specs=out_specs,
        out_shape=out_shapes,
        scratch_shapes=[pltpu.VMEM((SUBLANES, EV_QKV), F32)],
        compiler_params=_cparams(("arbitrary",)),
        name="even_in",
    )(x, nw, w, cw, alog_row, dtb_row, hist)


OD_K = 512
OD_V = 1024
OD_R = 2048
OD_G = 3072
OD_COLS = OD_G + LANES


def _odd_in_kernel(x_ref, nw_ref, w_ref, wg_ref, bg_ref, q_ref, k_ref, v_ref, r_ref, gl_ref):
    h = _rms(x_ref[...], nw_ref[...]).astype(BF16)
    q_ref[...] = _dot(h, w_ref[:, 0:512]) * (GLA_DK ** -0.5)
    k_ref[...] = _dot(h, w_ref[:, OD_K:OD_K + 512])
    for c in range(2):
        v_ref[:, c * 512:(c + 1) * 512] = _dot(h, w_ref[:, OD_V + c * 512:OD_V + (c + 1) * 512])
        r_ref[:, c * 512:(c + 1) * 512] = _dot(h, w_ref[:, OD_R + c * 512:OD_R + (c + 1) * 512])
    gd = _dot(h, w_ref[:, OD_G:OD_G + LANES])
    x = _dot(gd.astype(BF16), wg_ref[...]) + bg_ref[...]
    gl_ref[...] = _log_sigmoid(x) * (1.0 / GLA_TAU)


def _odd_in(x, nw, w, wg, bg, *, tm):
    n, d = x.shape
    row = lambda c: pl.BlockSpec((tm, c), lambda i: (i, 0))
    cols = (512, 512, 1024, 1024, 512)
    return pl.pallas_call(
        _odd_in_kernel,
        grid=(n // tm,),
        in_specs=[row(d), _const_spec(nw.shape), _const_spec(w.shape), _const_spec(wg.shape), _const_spec(bg.shape)],
        out_specs=[row(c) for c in cols],
        out_shape=[jax.ShapeDtypeStruct((n, c), F32) for c in cols],
        compiler_params=_cparams(("parallel",)),
        name="odd_in",
    )(x, nw, w, wg, bg)


FF_CHUNK = 512


def _post_mlp_kernel(*refs, even, final):
    if even:
        x_ref, odn_ref, z_ref, osw_ref, gn_ref, wo_ref, nm_ref, wu_ref, wd_ref = refs[:9]
        rest = refs[9:]
    else:
        x_ref, o_ref, r_ref, gn_ref, wo_ref, nm_ref, wu_ref, wd_ref = refs[:8]
        rest = refs[8:]
    if final:
        nf_ref, y_ref = rest
    else:
        (y_ref,) = rest

    x = x_ref[...]
    gn = gn_ref[...]
    if even:
        parts = []
        for hh in range(DN_HEADS):
            sl = slice(hh * DN_DV, (hh + 1) * DN_DV)
            parts.append(_rms(odn_ref[:, sl], gn) * _silu(z_ref[:, sl]))
        a = jnp.concatenate(parts, axis=1).astype(BF16)
        nv = DN_HEADS * DN_DV
        mix = _dot(a, wo_ref[0:nv, :]) + _dot(osw_ref[...].astype(BF16), wo_ref[nv:, :])
    else:
        parts = []
        for hh in range(GLA_HEADS):
            sl = slice(hh * GLA_DV, (hh + 1) * GLA_DV)
            parts.append(_rms(o_ref[:, sl], gn) * _silu(r_ref[:, sl]))
        mix = _dot(jnp.concatenate(parts, axis=1).astype(BF16), wo_ref[...])
    x1 = x + mix
    h2 = _rms(x1, nm_ref[...]).astype(BF16)
    d_ff = wu_ref.shape[1]
    acc = x1
    for c in range(d_ff // FF_CHUNK):
        sl = slice(c * FF_CHUNK, (c + 1) * FF_CHUNK)
        up = jnp.maximum(_dot(h2, wu_ref[:, sl]), 0.0)
        acc = acc + _dot((up * up).astype(BF16), wd_ref[sl, :])
    if final:
        acc = _rms(acc, nf_ref[...])
    y_ref[...] = acc


def _post_mlp(x, mixer_outs, gn, wo, nm, wu, wd, nf, *, tm, even):
    n, d = x.shape
    row = lambda c: pl.BlockSpec((tm, c), lambda i: (i, 0))
    final = nf is not None
    ins = [x, *mixer_outs, gn, wo, nm, wu, wd]
    specs = [row(d)] + [row(a.shape[1]) for a in mixer_outs] + [_const_spec(a.shape) for a in (gn, wo, nm, wu, wd)]
    if final:
        ins.append(nf)
        specs.append(_const_spec(nf.shape))
    return pl.pallas_call(
        functools.partial(_post_mlp_kernel, even=even, final=final),
        grid=(n // tm,),
        in_specs=specs,
        out_specs=row(d),
        out_shape=jax.ShapeDtypeStruct((n, d), F32),
        compiler_params=_cparams(("parallel",)),
        name="post_mlp_even" if even else "post_mlp_odd",
    )(*ins)


def _dn_prompt_kernel(q_ref, k_ref, v_ref, gbc_ref, gbr_ref, o_ref, s_ref, st_ref, *, chunks):
    t = pl.program_id(1)

    @pl.when(t == 0)
    def _():
        st_ref[...] = jnp.zeros_like(st_ref)

    C = CHUNK
    ri = lax.broadcasted_iota(jnp.int32, (C, C), 0)
    ci = lax.broadcasted_iota(jnp.int32, (C, C), 1)
    tri_incl = ri >= ci
    tri_strict = ri > ci
    tri_f = tri_incl.astype(F32)
    triT_f = (ri <= ci).astype(F32)
    eye = (ri == ci).astype(F32)

    def body(c, carry):
        r0 = pl.multiple_of(c * C, C)
        gbc = gbc_ref[pl.ds(r0, C), :]
        gbr = gbr_ref[c]
        Gcol = _dot_f32(tri_f, gbc)
        Grow = _dot_f32(gbr, triT_f)
        for hh in range(DN_HEADS):
            sl = slice(hh * DN_DK, (hh + 1) * DN_DK)
            q = q_ref[pl.ds(r0, C), sl]
            k = k_ref[pl.ds(r0, C), sl]
            v = v_ref[pl.ds(r0, C), sl]
            Gc = Gcol[:, hh:hh + 1]
            Gr = Grow[hh:hh + 1, :]
            beta = gbc[:, DN_HEADS + hh:DN_HEADS + hh + 1]
            decay = jnp.where(tri_incl, jnp.exp(jnp.where(tri_incl, Gc - Gr, 0.0)), 0.0)
            kb = k * beta
            k16 = k.astype(BF16)
            q16 = q.astype(BF16)
            M = jnp.where(tri_strict, _dot_nt(kb.astype(BF16), k16) * decay, 0.0)
            X = eye - M
            P = M
            n_sq = max((C - 1).bit_length() - 1, 0)
            for _ in range(n_sq):
                P16 = P.astype(BF16)
                P = _dot(P16, P16)
                X = X + _dot(X.astype(BF16), P.astype(BF16))
            eG = jnp.exp(Gc)
            rhs = jnp.concatenate([v * beta, kb * eG], axis=1).astype(BF16)
            uw = _dot(X.astype(BF16), rhs)
            u = uw[:, 0:DN_DV]
            w = uw[:, DN_DV:]
            S = st_ref[hh]
            lhs = jnp.concatenate([w, q * eG], axis=0).astype(BF16)
            R = _dot(lhs, S.astype(BF16))
            v_new = u - R[0:C]
            attn = _dot_nt(q16, k16) * decay
            vn16 = v_new.astype(BF16)
            o_ref[pl.ds(r0, C), sl] = R[C:] + _dot(attn.astype(BF16), vn16)
            g_last = Gc[C - 1:C, :]
            khat = (k * jnp.exp(g_last - Gc)).astype(BF16)
            st_ref[hh] = S * jnp.exp(g_last) + _dot_tn(khat, vn16)
        return carry

    lax.fori_loop(0, chunks, body, 0)

    @pl.when(t == pl.num_programs(1) - 1)
    def _():
        s_ref[0] = st_ref[...]


def _dn_prompt(q, k, v, gb, *, batch, seq_len, tc):
    n = q.shape[0]
    nt = seq_len // tc
    chunks = tc // CHUNK
    gbr = jnp.swapaxes(gb.reshape(n // CHUNK, CHUNK, 2 * DN_HEADS), 1, 2)
    row = pl.BlockSpec((tc, DN_HEADS * DN_DK), lambda b, t: (b * nt + t, 0))
    return pl.pallas_call(
        functools.partial(_dn_prompt_kernel, chunks=chunks),
        grid=(batch, nt),
        in_specs=[row, row, row,
                  pl.BlockSpec((tc, 2 * DN_HEADS), lambda b, t: (b * nt + t, 0)),
                  pl.BlockSpec((chunks, 2 * DN_HEADS, CHUNK), lambda b, t: (b * nt + t, 0, 0))],
        out_specs=[row, pl.BlockSpec((1, DN_HEADS, DN_DK, DN_DV), lambda b, t: (b, 0, 0, 0))],
        out_shape=[jax.ShapeDtypeStruct((n, DN_HEADS * DN_DV), F32),
                   jax.ShapeDtypeStruct((batch, DN_HEADS, DN_DK, DN_DV), F32)],
        scratch_shapes=[pltpu.VMEM((DN_HEADS, DN_DK, DN_DV), F32)],
        compiler_params=_cparams(("parallel", "arbitrary")),
        name="dn_prompt",
    )(q, k, v, gb, gbr)


def _gla_prompt_kernel(q_ref, k_ref, v_ref, gl_ref, o_ref, s_ref, st_ref, *, chunks):
    t = pl.program_id(1)

    @pl.when(t == 0)
    def _():
        st_ref[...] = jnp.zeros_like(st_ref)

    C = CHUNK
    nsub = C // SUB
    ri = lax.broadcasted_iota(jnp.int32, (C, C), 0)
    ci = lax.broadcasted_iota(jnp.int32, (C, C), 1)
    tri_f = (ri >= ci).astype(F32)
    blk_lo = (ri // SUB) * SUB
    rloc = ri - blk_lo

    def body(c, carry):
        r0 = pl.multiple_of(c * C, C)
        Gall = _dot_f32(tri_f, gl_ref[pl.ds(r0, C), :])
        for hh in range(GLA_HEADS):
            ks = slice(hh * GLA_DK, (hh + 1) * GLA_DK)
            vs = slice(hh * GLA_DV, (hh + 1) * GLA_DV)
            q = q_ref[pl.ds(r0, C), ks]
            k = k_ref[pl.ds(r0, C), ks]
            v16 = v_ref[pl.ds(r0, C), vs].astype(BF16)
            G = Gall[:, ks]
            St = st_ref[hh]
            o = _dot_nt((q * jnp.exp(G)).astype(BF16), St.astype(BF16))
            qparts, kparts = [], []
            for I in range(1, nsub):
                lo = I * SUB
                Gr = G[lo:lo + 1, :]
                qI = q[lo:lo + SUB] * jnp.exp(G[lo:lo + SUB] - Gr)
                kI = k[0:lo] * jnp.exp(Gr - G[0:lo])
                qparts.append(jnp.concatenate(
                    ([jnp.zeros((lo, GLA_DK), F32)] if lo else []) + [qI] +
                    ([jnp.zeros((C - lo - SUB, GLA_DK), F32)] if C - lo - SUB else []), axis=0))
                kparts.append(jnp.concatenate([kI, jnp.zeros((C - lo, GLA_DK), F32)], axis=0))
            A = _dot_nt(jnp.concatenate(qparts, axis=1).astype(BF16),
                        jnp.concatenate(kparts, axis=1).astype(BF16))
            for j in range(SUB):
                ksel = jnp.concatenate(
                    [jnp.broadcast_to(k[I * SUB + j:I * SUB + j + 1], (SUB, GLA_DK)) for I in range(nsub)], axis=0)
                gsel = jnp.concatenate(
                    [jnp.broadcast_to(G[I * SUB + j:I * SUB + j + 1], (SUB, GLA_DK)) for I in range(nsub)], axis=0)
                col = jnp.sum(q * ksel * jnp.exp(jnp.minimum(G - gsel, 0.0)), axis=-1, keepdims=True)
                A = jnp.where((ci == blk_lo + j) & (rloc >= j), col, A)
            o_ref[pl.ds(r0, C), vs] = o + _dot(A.astype(BF16), v16)
            gl = G[C - 1:C, :]
            khat = (k * jnp.exp(gl - G)).astype(BF16)
            st_ref[hh] = St * jnp.exp(gl) + _dot_tn(v16, khat)
        return carry

    lax.fori_loop(0, chunks, body, 0)

    @pl.when(t == pl.num_programs(1) - 1)
    def _():
        s_ref[0] = st_ref[...]


def _gla_prompt(q, k, v, gl, *, batch, seq_len, tc):
    n = q.shape[0]
    nt = seq_len // tc
    chunks = tc // CHUNK
    rowk = pl.BlockSpec((tc, GLA_HEADS * GLA_DK), lambda b, t: (b * nt + t, 0))
    rowv = pl.BlockSpec((tc, GLA_HEADS * GLA_DV), lambda b, t: (b * nt + t, 0))
    return pl.pallas_call(
        functools.partial(_gla_prompt_kernel, chunks=chunks),
        grid=(batch, nt),
        in_specs=[rowk, rowk, rowv, rowk],
        out_specs=[rowv, pl.BlockSpec((1, GLA_HEADS, GLA_DV, GLA_DK), lambda b, t: (b, 0, 0, 0))],
        out_shape=[jax.ShapeDtypeStruct((n, GLA_HEADS * GLA_DV), F32),
                   jax.ShapeDtypeStruct((batch, GLA_HEADS, GLA_DV, GLA_DK), F32)],
        scratch_shapes=[pltpu.VMEM((GLA_HEADS, GLA_DV, GLA_DK), F32)],
        compiler_params=_cparams(("parallel", "arbitrary")),
        name="gla_prompt",
    )(q, k, v, gl)


def _alibi_slope(h):
    return 2.0 ** (-8.0 * (h + 1) / SW_HEADS)


def _swa_prompt_kernel(sink_ref, q_ref, kc_ref, vc_ref, kp_ref, vp_ref, o_ref):
    n = pl.program_id(1)
    W = WINDOW
    ri = lax.broadcasted_iota(jnp.int32, (W, 2 * W), 0)
    ci = lax.broadcasted_iota(jnp.int32, (W, 2 * W), 1)
    rel = ri + W - ci
    valid = (rel >= 0) & (rel < W) & ((ci >= W) | (n > 0))
    relf = rel.astype(F32)
    for kh in range(SW_KV_HEADS):
        sl = slice(kh * SW_HD, (kh + 1) * SW_HD)
        K = jnp.concatenate([kp_ref[:, sl], kc_ref[:, sl]], axis=0).astype(BF16)
        V = jnp.concatenate([vp_ref[:, sl], vc_ref[:, sl]], axis=0).astype(BF16)
        for g in range(SW_GROUP):
            h = kh * SW_GROUP + g
            qh = q_ref[:, h * SW_HD:(h + 1) * SW_HD].astype(BF16)
            s = _dot_nt(qh, K) * (SW_HD ** -0.5) - _alibi_slope(h) * relf
            s = jnp.where(valid, s, -jnp.inf)
            sink = sink_ref[h]
            m = jnp.maximum(jnp.max(s, axis=-1, keepdims=True), sink)
            p = jnp.exp(s - m)
            den = jnp.sum(p, axis=-1, keepdims=True) + jnp.exp(sink - m)
            p = p / den
            o_ref[:, h * SW_HD:(h + 1) * SW_HD] = _dot(p.astype(BF16), V)


def _swa_prompt(q, k, v, sinks, *, batch, seq_len):
    n = q.shape[0]
    nb = seq_len // WINDOW
    cur = lambda c: pl.BlockSpec((WINDOW, c), lambda b, t: (b * nb + t, 0))
    prev = lambda c: pl.BlockSpec((WINDOW, c), lambda b, t: (b * nb + jnp.maximum(t - 1, 0), 0))
    kvc = SW_KV_HEADS * SW_HD
    return pl.pallas_call(
        _swa_prompt_kernel,
        grid=(batch, nb),
        in_specs=[pl.BlockSpec(memory_space=pltpu.SMEM), cur(SW_HEADS * SW_HD), cur(kvc), cur(kvc), prev(kvc), prev(kvc)],
        out_specs=cur(SW_HEADS * SW_HD),
        out_shape=jax.ShapeDtypeStruct((n, SW_HEADS * SW_HD), F32),
        compiler_params=_cparams(("parallel", "parallel")),
        name="swa_prompt",
    )(sinks, q, k, v, k, v)


SEQ_BLK = 8


def _dn_decode_kernel(gb_ref, s_ref, qT_ref, kT_ref, v_ref, o_ref, sn_ref):
    i = pl.program_id(0)
    for j in range(SEQ_BLK):
        for hh in range(DN_HEADS):
            sl = slice(hh * DN_DK, (hh + 1) * DN_DK)
            S = s_ref[j, hh]
            kcol = kT_ref[0, sl, j:j + 1]
            qcol = qT_ref[0, sl, j:j + 1]
            vrow = v_ref[j:j + 1, sl]
            a = jnp.exp(jnp.full((1, DN_DV), gb_ref[i * SEQ_BLK + j, hh], F32))
            beta = gb_ref[i * SEQ_BLK + j, DN_HEADS + hh]
            Sa = S * a
            pred = jnp.sum(Sa * kcol, axis=0, keepdims=True)
            Sn = Sa + kcol * (beta * (vrow - pred))
            sn_ref[j, hh] = Sn
            o_ref[j:j + 1, sl] = jnp.sum(Sn * qcol, axis=0, keepdims=True)


def _cols_by_block(a):
    n, c = a.shape
    return jnp.swapaxes(a.reshape(n // SEQ_BLK, SEQ_BLK, c), 1, 2)


def _dn_decode(S, q, k, v, gb):
    n = q.shape[0]
    hd = DN_HEADS * DN_DK
    colspec = pl.BlockSpec((1, hd, SEQ_BLK), lambda i: (i, 0, 0))
    sspec = pl.BlockSpec((SEQ_BLK, DN_HEADS, DN_DK, DN_DV), lambda i: (i, 0, 0, 0))
    rspec = pl.BlockSpec((SEQ_BLK, hd), lambda i: (i, 0))
    return pl.pallas_call(
        _dn_decode_kernel,
        grid=(n // SEQ_BLK,),
        in_specs=[pl.BlockSpec(memory_space=pltpu.SMEM), sspec, colspec, colspec, rspec],
        out_specs=[rspec, sspec],
        out_shape=[jax.ShapeDtypeStruct((n, hd), F32), jax.ShapeDtypeStruct(S.shape, F32)],
        compiler_params=_cparams(("parallel",)),
        name="dn_decode",
    )(gb, S, _cols_by_block(q), _cols_by_block(k), v)


def _gla_decode_kernel(s_ref, qT_ref, kT_ref, glT_ref, v_ref, o_ref, sn_ref):
    for j in range(SEQ_BLK):
        for hh in range(GLA_HEADS):
            ks = slice(hh * GLA_DK, (hh + 1) * GLA_DK)
            vs = slice(hh * GLA_DV, (hh + 1) * GLA_DV)
            S = s_ref[j, hh]
            kcol = kT_ref[0, ks, j:j + 1]
            qcol = qT_ref[0, ks, j:j + 1]
            acol = jnp.exp(glT_ref[0, ks, j:j + 1])
            vrow = v_ref[j:j + 1, vs]
            Sn = S * acol + kcol * vrow
            sn_ref[j, hh] = Sn
            o_ref[j:j + 1, vs] = jnp.sum(Sn * qcol, axis=0, keepdims=True)


def _gla_decode(S, q, k, v, gl):
    n = q.shape[0]
    hk = GLA_HEADS * GLA_DK
    hv = GLA_HEADS * GLA_DV
    colspec = pl.BlockSpec((1, hk, SEQ_BLK), lambda i: (i, 0, 0))
    sspec = pl.BlockSpec((SEQ_BLK, GLA_HEADS, GLA_DK, GLA_DV), lambda i: (i, 0, 0, 0))
    rspec = pl.BlockSpec((SEQ_BLK, hv), lambda i: (i, 0))
    return pl.pallas_call(
        _gla_decode_kernel,
        grid=(n // SEQ_BLK,),
        in_specs=[sspec, colspec, colspec, colspec, rspec],
        out_specs=[rspec, sspec],
        out_shape=[jax.ShapeDtypeStruct((n, hv), F32), jax.ShapeDtypeStruct(S.shape, F32)],
        compiler_params=_cparams(("parallel",)),
        name="gla_decode",
    )(S, _cols_by_block(q), _cols_by_block(k), _cols_by_block(gl), v)


def _swa_decode_kernel(sink_ref, q_ref, kn_ref, vn_ref, kc_ref, vc_ref, o_ref, ko_ref, vo_ref):
    W = WINDOW
    wi = lax.broadcasted_iota(jnp.int32, (SW_GROUP, W), 1)
    relf = (W - wi).astype(F32)
    valid = wi >= 1
    gi = lax.broadcasted_iota(jnp.int32, (SW_GROUP, 1), 0)
    for j in range(SEQ_BLK):
        ko_ref[j, 0:W - 1, :] = kc_ref[j, 1:W, :]
        ko_ref[j, W - 1:W, :] = kn_ref[j:j + 1, :]
        vo_ref[j, 0:W - 1, :] = vc_ref[j, 1:W, :]
        vo_ref[j, W - 1:W, :] = vn_ref[j:j + 1, :]
        for kh in range(SW_KV_HEADS):
            sl = slice(kh * SW_HD, (kh + 1) * SW_HD)
            Kc = kc_ref[j, :, sl].astype(BF16)
            Vc = vc_ref[j, :, sl].astype(BF16)
            knew = kn_ref[j:j + 1, sl]
            vnew = vn_ref[j:j + 1, sl]
            qg = jnp.concatenate(
                [q_ref[j:j + 1, (kh * SW_GROUP + g) * SW_HD:(kh * SW_GROUP + g + 1) * SW_HD] for g in range(SW_GROUP)],
                axis=0)
            slope = jnp.zeros((SW_GROUP, 1), F32)
            sink = jnp.zeros((SW_GROUP, 1), F32)
            for g in range(SW_GROUP):
                slope = jnp.where(gi == g, _alibi_slope(kh * SW_GROUP + g), slope)
                sink = jnp.where(gi == g, sink_ref[kh * SW_GROUP + g], sink)
            qg16 = qg.astype(BF16)
            s = _dot_nt(qg16, Kc) * (SW_HD ** -0.5) - slope * relf
            s = jnp.where(valid, s, -jnp.inf)
            s_new = jnp.sum(qg16.astype(F32) * knew.astype(BF16).astype(F32), axis=-1, keepdims=True) * (SW_HD ** -0.5)
            m = jnp.maximum(jnp.maximum(jnp.max(s, axis=-1, keepdims=True), s_new), sink)
            p = jnp.exp(s - m)
            p_new = jnp.exp(s_new - m)
            den = jnp.sum(p, axis=-1, keepdims=True) + p_new + jnp.exp(sink - m)
            og = (_dot((p / den).astype(BF16), Vc)
                  + (p_new / den).astype(BF16).astype(F32) * vnew.astype(BF16).astype(F32))
            for g in range(SW_GROUP):
                h = kh * SW_GROUP + g
                o_ref[j:j + 1, h * SW_HD:(h + 1) * SW_HD] = og[g:g + 1]


def _swa_decode(q, kn, vn, kc, vc, sinks):
    n = q.shape[0]
    kvc = SW_KV_HEADS * SW_HD
    rq = pl.BlockSpec((SEQ_BLK, SW_HEADS * SW_HD), lambda i: (i, 0))
    rkv = pl.BlockSpec((SEQ_BLK, kvc), lambda i: (i, 0))
    cs = pl.BlockSpec((SEQ_BLK, WINDOW, kvc), lambda i: (i, 0, 0))
    return pl.pallas_call(
        _swa_decode_kernel,
        grid=(n // SEQ_BLK,),
        in_specs=[pl.BlockSpec(memory_space=pltpu.SMEM), rq, rkv, rkv, cs, cs],
        out_specs=[rq, cs, cs],
        out_shape=[jax.ShapeDtypeStruct((n, SW_HEADS * SW_HD), F32),
                   jax.ShapeDtypeStruct(kc.shape, F32), jax.ShapeDtypeStruct(vc.shape, F32)],
        compiler_params=_cparams(("parallel",)),
        name="swa_decode",
    )(sinks, q, kn, vn, kc, vc)


def _pad_cols(a, width):
    return jnp.pad(a, ((0, 0), (0, width - a.shape[1])))


def _even_weight(w):
    ab0 = EV_QKV + 512
    sw0 = ab0 + 2 * DN_HEADS
    return jnp.concatenate([w[:, :ab0], w[:, sw0:], _pad_cols(w[:, ab0:sw0], LANES)], axis=1).astype(BF16)


def _odd_weight(w):
    return jnp.concatenate([w[:, :OD_G], _pad_cols(w[:, OD_G:], LANES)], axis=1).astype(BF16)


def _row(a, width=None):
    a = a.reshape(1, -1).astype(F32)
    return a if width is None else _pad_cols(a, width)


def kernel(x_prompt, x_sample, state_dn, state_dn_conv, cache_swa_k, cache_swa_v, state_gla,
           norm_mix, norm_mlp, norm_final,
           even_w_in, dn_conv_w, dn_a_log, dn_dt_bias, dn_norm, sw_sinks, even_w_out,
           gla_w_in, gla_w_gate_up, gla_b_gate, gla_norm, gla_w_out,
           mlp_w_up, mlp_w_down):
    B, L, D = x_prompt.shape
    NS = x_sample.shape[0]
    depth = norm_mix.shape[0]
    kvc = SW_KV_HEADS * SW_HD
    xp = x_prompt.reshape(B * L, D)
    xs = x_sample.reshape(NS, D)
    tm_p = 512
    tc_p = 512
    no_hist = jnp.zeros((SUBLANES, LANES), F32)

    p_dn, p_conv, p_k, p_v, p_gla = [], [], [], [], []
    s_dn, s_conv, s_k, s_v, s_gla = [], [], [], [], []
    for layer in range(depth):
        last = layer == depth - 1
        nf = _row(norm_final) if last else None
        nmix = _row(norm_mix[layer])
        nmlp = _row(norm_mlp[layer])
        wu = mlp_w_up[layer].astype(BF16)
        wd = mlp_w_down[layer].astype(BF16)
        if layer % 2 == 0:
            e = layer // 2
            w_in = _even_weight(even_w_in[e])
            cw = dn_conv_w[e]
            alog = _row(dn_a_log[e], LANES)
            dtb = _row(dn_dt_bias[e], LANES)
            gn = _row(dn_norm[e])
            wo = even_w_out[e].astype(BF16)
            sinks = sw_sinks[e].astype(F32)
            q, k, v, z, gb, qsw, ksw, vsw, upre = _even_in(
                xp, nmix, w_in, cw, alog, dtb, no_hist, tm=tm_p, seq_len=L, per_token_hist=False)
            o_dn, S = _dn_prompt(q, k, v, gb, batch=B, seq_len=L, tc=tc_p)
            o_sw = _swa_prompt(qsw, ksw, vsw, sinks, batch=B, seq_len=L)
            xp = _post_mlp(xp, (o_dn, z, o_sw), gn, wo, nmlp, wu, wd, nf, tm=tm_p, even=True)
            p_dn.append(S)
            p_conv.append(upre.reshape(B, L, EV_QKV)[:, L - (DN_CONV - 1):])
            p_k.append(ksw.reshape(B, L, SW_KV_HEADS, SW_HD)[:, L - WINDOW:])
            p_v.append(vsw.reshape(B, L, SW_KV_HEADS, SW_HD)[:, L - WINDOW:])
            hist = jnp.swapaxes(state_dn_conv[e], 0, 1)
            q, k, v, z, gb, qsw, ksw, vsw, upre = _even_in(
                xs, nmix, w_in, cw, alog, dtb, hist, tm=NS, seq_len=1, per_token_hist=True)
            o_dn, S = _dn_decode(state_dn[e], q, k, v, gb)
            o_sw, nk, nv = _swa_decode(qsw, ksw, vsw, cache_swa_k[e].reshape(NS, WINDOW, kvc),
                                       cache_swa_v[e].reshape(NS, WINDOW, kvc), sinks)
            xs = _post_mlp(xs, (o_dn, z, o_sw), gn, wo, nmlp, wu, wd, nf, tm=NS, even=True)
            s_dn.append(S)
            s_conv.append(jnp.concatenate([state_dn_conv[e][:, 1:], upre[:, None, :]], axis=1))
            s_k.append(nk.reshape(NS, WINDOW, SW_KV_HEADS, SW_HD))
            s_v.append(nv.reshape(NS, WINDOW, SW_KV_HEADS, SW_HD))
        else:
            o_i = layer // 2
            w_in = _odd_weight(gla_w_in[o_i])
            wg = jnp.pad(gla_w_gate_up[o_i], ((0, LANES - GLA_RANK), (0, 0))).astype(BF16)
            bg = _row(gla_b_gate[o_i])
            gn = _row(gla_norm[o_i])
            wo = gla_w_out[o_i].astype(BF16)
            q, k, v, r, gl = _odd_in(xp, nmix, w_in, wg, bg, tm=tm_p)
            o, St = _gla_prompt(q, k, v, gl, batch=B, seq_len=L, tc=tc_p)
            xp = _post_mlp(xp, (o, r), gn, wo, nmlp, wu, wd, nf, tm=tm_p, even=False)
            p_gla.append(jnp.swapaxes(St, 2, 3))
            q, k, v, r, gl = _odd_in(xs, nmix, w_in, wg, bg, tm=NS)
            o, S = _gla_decode(state_gla[o_i], q, k, v, gl)
            xs = _post_mlp(xs, (o, r), gn, wo, nmlp, wu, wd, nf, tm=NS, even=False)
            s_gla.append(S)

    return (xp.reshape(B, L, D), xs.reshape(NS, 1, D),
            jnp.stack(p_dn), jnp.stack(p_conv), jnp.stack(p_k), jnp.stack(p_v), jnp.stack(p_gla),
            jnp.stack(s_dn), jnp.stack(s_conv), jnp.stack(s_k), jnp.stack(s_v), jnp.stack(s_gla))
```

```python
import functools

import jax
import jax.numpy as jnp
from jax import lax
from jax.experimental import pallas as pl
from jax.experimental.pallas import tpu as pltpu

F32 = jnp.float32
BF16 = jnp.bfloat16
EPS = 1e-6

DN_HEADS = 4
DN_DK = 128
DN_DV = 128
DN_CONV = 4
SW_HEADS = 8
SW_KV_HEADS = 2
SW_GROUP = SW_HEADS // SW_KV_HEADS
SW_HD = 64
WINDOW = 128
GLA_HEADS = 4
GLA_DK = 128
GLA_DV = 256
GLA_RANK = 16
GLA_TAU = 16.0

LANES = 128
SUBLANES = 8
CHUNK = 64
SUB = 16
VMEM_LIMIT = 56 * 1024 * 1024
CAT = DN_HEADS * CHUNK


def _dot(a, b):
    return jnp.dot(a, b, preferred_element_type=F32)


def _dot_nt(a, b):
    return lax.dot_general(a, b, (((1,), (1,)), ((), ())), preferred_element_type=F32)


def _dot_tn(a, b):
    return lax.dot_general(a, b, (((0,), (0,)), ((), ())), preferred_element_type=F32)


def _dot_f32(a, b):
    return jnp.dot(a, b, preferred_element_type=F32, precision=lax.Precision.HIGHEST)


def _rms(x, g):
    return x * lax.rsqrt(jnp.mean(x * x, axis=-1, keepdims=True) + EPS) * g


def _silu(x):
    return x * jax.nn.sigmoid(x)


def _softplus(x):
    return jnp.maximum(x, 0.0) + jnp.log1p(jnp.exp(-jnp.abs(x)))


def _log_sigmoid(x):
    return jnp.minimum(x, 0.0) - jnp.log1p(jnp.exp(-jnp.abs(x)))


def _cparams(sem):
    return pltpu.CompilerParams(dimension_semantics=sem, vmem_limit_bytes=VMEM_LIMIT)


def _const_spec(shape):
    nd = len(shape)
    return pl.BlockSpec(shape, lambda *_: (0,) * nd)


def _layer_spec(a, layer):
    nd = a.ndim - 1
    return pl.BlockSpec((None,) + a.shape[1:], lambda *_: (layer,) + (0,) * nd)


def _seg_cumsum(x, seg, axis):
    pos = lax.broadcasted_iota(jnp.int32, x.shape, axis) & (seg - 1)
    s = 1
    while s < seg:
        x = x + jnp.where(pos >= s, pltpu.roll(x, s, axis), 0.0)
        s *= 2
    return x


EV_QKV = 3 * DN_HEADS * DN_DK
EV_Z = EV_QKV
EV_QSW = EV_Z + 512
EV_KSW = EV_QSW + 512
EV_VSW = EV_KSW + 128
EV_AB = EV_VSW + 128
EV_AX = EV_AB + LANES
EV_BX = EV_AX + CAT
EV_COLS = EV_BX + CAT


def _even_in_kernel(x_ref, nw_ref, w_ref, cw_ref, alog_ref, dtb_ref, alogx_ref, dtbx_ref, hist_ref,
                    q_ref, k_ref, v_ref, z_ref, gb_ref, gx_ref, bx_ref, qsw_ref, ksw_ref, vsw_ref, upre_ref,
                    carry_ref, *, tiles_per_seq, per_token_hist):
    i = pl.program_id(0)
    tm = x_ref.shape[0]
    h = _rms(x_ref[...], nw_ref[...]).astype(BF16)

    if not per_token_hist:
        @pl.when(i % tiles_per_seq == 0)
        def _():
            carry_ref[...] = jnp.zeros_like(carry_ref)

    row8 = lax.broadcasted_iota(jnp.int32, (SUBLANES, 512), 0)

    def conv_section(u, c0):
        cw = cw_ref[:, c0:c0 + 512]
        if per_token_hist:
            upre_ref[:, c0:c0 + 512] = u
            out = u * cw[3:4]
            for t in range(DN_CONV - 1):
                out = out + hist_ref[t, :, c0:c0 + 512] * cw[t:t + 1]
            return out
        prev = carry_ref[:, c0:c0 + 512]
        out = u * cw[3:4]
        for s in range(1, DN_CONV):
            sh = pltpu.roll(u, s, 0)
            top = jnp.where(row8 < s, pltpu.roll(prev, s, 0), sh[0:SUBLANES])
            sh = jnp.concatenate([top, sh[SUBLANES:]], axis=0)
            out = out + sh * cw[3 - s:4 - s]
        carry_ref[:, c0:c0 + 512] = u[tm - SUBLANES:tm]
        upre_ref[:, c0:c0 + 512] = u[tm - SUBLANES:tm]
        return out

    def l2n(y):
        outs = []
        for hh in range(DN_HEADS):
            yh = y[:, hh * DN_DK:(hh + 1) * DN_DK]
            outs.append(yh * lax.rsqrt(jnp.sum(yh * yh, axis=-1, keepdims=True) + EPS))
        return jnp.concatenate(outs, axis=1)

    q_ref[...] = l2n(_silu(conv_section(_dot(h, w_ref[:, 0:512]), 0))) * (DN_DK ** -0.5)
    k_ref[...] = l2n(_silu(conv_section(_dot(h, w_ref[:, 512:1024]), 512)))
    v_ref[...] = _silu(conv_section(_dot(h, w_ref[:, 1024:1536]), 1024))
    z_ref[...] = _dot(h, w_ref[:, EV_Z:EV_Z + 512])
    qsw_ref[...] = _dot(h, w_ref[:, EV_QSW:EV_QSW + 512])
    ksw_ref[...] = _dot(h, w_ref[:, EV_KSW:EV_KSW + 128])
    vsw_ref[...] = _dot(h, w_ref[:, EV_VSW:EV_VSW + 128])
    ab = _dot(h, w_ref[:, EV_AB:EV_AB + LANES])
    g = -jnp.exp(alog_ref[...]) * _softplus(ab + dtb_ref[...])
    lane = lax.broadcasted_iota(jnp.int32, ab.shape, 1)
    gb = jnp.where(lane < DN_HEADS, g, jax.nn.sigmoid(ab))
    gb_ref[...] = gb[:, 0:2 * DN_HEADS]
    ax = _dot(h, w_ref[:, EV_AX:EV_AX + CAT])
    gx_ref[...] = -jnp.exp(alogx_ref[...]) * _softplus(ax + dtbx_ref[...])
    bx_ref[...] = jax.nn.sigmoid(_dot(h, w_ref[:, EV_BX:EV_BX + CAT]))


def _even_in(x, nw, layer, w, cw, alog, dtb, alogx, dtbx, sub, hist, *, tm, seq_len, per_token_hist):
    n, d = x.shape
    nt = n // tm
    tiles_per_seq = max(seq_len // tm, 1)
    row = lambda c: pl.BlockSpec((tm, c), lambda i: (i, 0))
    cols = (512, 512, 512, 512, 2 * DN_HEADS, CAT, CAT, 512, 128, 128)
    out_shapes = [jax.ShapeDtypeStruct((n, c), F32) for c in cols]
    out_specs = [row(c) for c in cols]
    if per_token_hist:
        hist_spec = pl.BlockSpec((DN_CONV - 1, tm, EV_QKV), lambda i: (0, i, 0))
        out_shapes.append(jax.ShapeDtypeStruct((n, EV_QKV), F32))
        out_specs.append(row(EV_QKV))
    else:
        hist_spec = _const_spec(hist.shape)
        out_shapes.append(jax.ShapeDtypeStruct((nt, SUBLANES, EV_QKV), F32))
        out_specs.append(pl.BlockSpec((None, SUBLANES, EV_QKV), lambda i: (i, 0, 0)))
    return pl.pallas_call(
        functools.partial(_even_in_kernel, tiles_per_seq=tiles_per_seq, per_token_hist=per_token_hist),
        grid=(nt,),
        in_specs=([row(d), _layer_spec(nw, layer)] + [_layer_spec(a, sub) for a in (w, cw, alog, dtb, alogx, dtbx)]
                  + [hist_spec]),
        out_specs=out_specs,
        out_shape=out_shapes,
        scratch_shapes=[pltpu.VMEM((SUBLANES, EV_QKV), F32)],
        compiler_params=_cparams(("arbitrary",)),
        name="even_in",
    )(x, nw, w, cw, alog, dtb, alogx, dtbx, hist)


OD_K = 512
OD_V = 1024
OD_R = 2048
OD_G = 3072
OD_COLS = OD_G + LANES


def _odd_in_kernel(x_ref, nw_ref, w_ref, wg_ref, bg_ref, q_ref, k_ref, v_ref, r_ref, gl_ref):
    h = _rms(x_ref[...], nw_ref[...]).astype(BF16)
    q_ref[...] = _dot(h, w_ref[:, 0:512]) * (GLA_DK ** -0.5)
    k_ref[...] = _dot(h, w_ref[:, OD_K:OD_K + 512])
    for c in range(2):
        v_ref[:, c * 512:(c + 1) * 512] = _dot(h, w_ref[:, OD_V + c * 512:OD_V + (c + 1) * 512])
        r_ref[:, c * 512:(c + 1) * 512] = _dot(h, w_ref[:, OD_R + c * 512:OD_R + (c + 1) * 512])
    gd = _dot(h, w_ref[:, OD_G:OD_G + LANES])
    x = _dot(gd.astype(BF16), wg_ref[...]) + bg_ref[...]
    gl_ref[...] = _log_sigmoid(x) * (1.0 / GLA_TAU)


def _odd_in(x, nw, layer, w, wg, bg, sub, *, tm):
    n, d = x.shape
    row = lambda c: pl.BlockSpec((tm, c), lambda i: (i, 0))
    cols = (512, 512, 1024, 1024, 512)
    return pl.pallas_call(
        _odd_in_kernel,
        grid=(n // tm,),
        in_specs=[row(d), _layer_spec(nw, layer)] + [_layer_spec(a, sub) for a in (w, wg, bg)],
        out_specs=[row(c) for c in cols],
        out_shape=[jax.ShapeDtypeStruct((n, c), F32) for c in cols],
        compiler_params=_cparams(("parallel",)),
        name="odd_in",
    )(x, nw, w, wg, bg)


FF_CHUNK = 512


def _post_mlp_kernel(*refs, even, final):
    if even:
        x_ref, odn_ref, z_ref, osw_ref, gn_ref, wo_ref, nm_ref, wu_ref, wd_ref = refs[:9]
        rest = refs[9:]
    else:
        x_ref, o_ref, r_ref, gn_ref, wo_ref, nm_ref, wu_ref, wd_ref = refs[:8]
        rest = refs[8:]
    if final:
        nf_ref, y_ref = rest
    else:
        (y_ref,) = rest

    x = x_ref[...]
    gn = gn_ref[...]
    if even:
        parts = []
        for hh in range(DN_HEADS):
            sl = slice(hh * DN_DV, (hh + 1) * DN_DV)
            parts.append(_rms(odn_ref[:, sl], gn) * _silu(z_ref[:, sl]))
        a = jnp.concatenate(parts, axis=1).astype(BF16)
        nv = DN_HEADS * DN_DV
        mix = _dot(a, wo_ref[0:nv, :]) + _dot(osw_ref[...].astype(BF16), wo_ref[nv:, :])
    else:
        parts = []
        for hh in range(GLA_HEADS):
            sl = slice(hh * GLA_DV, (hh + 1) * GLA_DV)
            parts.append(_rms(o_ref[:, sl], gn) * _silu(r_ref[:, sl]))
        mix = _dot(jnp.concatenate(parts, axis=1).astype(BF16), wo_ref[...])
    x1 = x + mix
    h2 = _rms(x1, nm_ref[...]).astype(BF16)
    d_ff = wu_ref.shape[1]
    acc = x1
    for c in range(d_ff // FF_CHUNK):
        sl = slice(c * FF_CHUNK, (c + 1) * FF_CHUNK)
        up = jnp.maximum(_dot(h2, wu_ref[:, sl]), 0.0)
        acc = acc + _dot((up * up).astype(BF16), wd_ref[sl, :])
    if final:
        acc = _rms(acc, nf_ref[...])
    y_ref[...] = acc


def _post_mlp(x, mixer_outs, gn, wo, nm, wu, wd, nf, layer, sub, *, tm, even):
    n, d = x.shape
    row = lambda c: pl.BlockSpec((tm, c), lambda i: (i, 0))
    final = nf is not None
    ins = [x, *mixer_outs, gn, wo, nm, wu, wd]
    specs = ([row(d)] + [row(a.shape[1]) for a in mixer_outs]
             + [_layer_spec(gn, sub), _layer_spec(wo, sub)] + [_layer_spec(a, layer) for a in (nm, wu, wd)])
    if final:
        ins.append(nf)
        specs.append(_const_spec(nf.shape))
    return pl.pallas_call(
        functools.partial(_post_mlp_kernel, even=even, final=final),
        grid=(n // tm,),
        in_specs=specs,
        out_specs=row(d),
        out_shape=jax.ShapeDtypeStruct((n, d), F32),
        compiler_params=_cparams(("parallel",)),
        name="post_mlp_even" if even else "post_mlp_odd",
    )(*ins)


def _block_diag(p16, row_blk, col_blk):
    R, Cc = p16.shape
    t = jnp.concatenate([p16] * DN_HEADS, axis=0)
    ri = lax.broadcasted_iota(jnp.int32, t.shape, 0) // row_blk
    ci = lax.broadcasted_iota(jnp.int32, t.shape, 1) // col_blk
    return jnp.where(ri == ci, t, jnp.zeros_like(t))


def _widen(x):
    lane = lax.broadcasted_iota(jnp.int32, (x.shape[0], LANES), 1)
    outs = []
    for t in range(CAT // LANES):
        xt = x[:, t * LANES:(t + 1) * LANES]
        xr = pltpu.roll(xt, CHUNK, 1)
        outs.append(jnp.where(lane < CHUNK, xt, xr))
        outs.append(jnp.where(lane < CHUNK, xr, xt))
    return jnp.concatenate(outs, axis=1)


def _dn_prompt_kernel(q_ref, k_ref, v_ref, gx_ref, bx_ref, grow_ref, brow_ref, o_ref, s_ref, st_ref, *, chunks):
    t = pl.program_id(0)
    nb = q_ref.shape[0]

    @pl.when(t == 0)
    def _():
        st_ref[...] = jnp.zeros_like(st_ref)

    C = CHUNK
    ri = lax.broadcasted_iota(jnp.int32, (C, CAT), 0)
    cj = lax.broadcasted_iota(jnp.int32, (C, CAT), 1) & (C - 1)
    tri_incl = ri >= cj
    tri_strict = ri > cj
    eye = (ri == cj).astype(F32)
    n_sq = max((C - 1).bit_length() - 1, 0)

    Gcols = [_seg_cumsum(gx_ref[b], C, 0) for b in range(nb)]
    Grows = [_seg_cumsum(grow_ref[b], C, 1) for b in range(nb)]
    ids = [(b, c) for c in range(chunks) for b in range(nb)]
    U = {}
    for (b, c) in ids:
        rs = slice(c * C, (c + 1) * C)
        Gc = Gcols[b][rs]
        Gr = Grows[b][c:c + 1]
        decay = jnp.where(tri_incl, jnp.exp(jnp.where(tri_incl, Gc - Gr, 0.0)), 0.0)
        k = k_ref[b, rs, :]
        q = q_ref[b, rs, :]
        k16 = k.astype(BF16)
        kq = _dot_nt(jnp.concatenate([k16, q.astype(BF16)], axis=0), _block_diag(k16, C, DN_DK))
        M = jnp.where(tri_strict, kq[0:C] * bx_ref[b, rs, :] * decay, 0.0)
        U[b, c] = dict(Gc=Gc, Gr=Gr, k=k, q=q, k16=k16, attn=kq[C:] * decay, X=eye - M, P16=M.astype(BF16))
    for (b, c) in ids:
        u_ = U[b, c]
        u_["P16"] = _dot(u_["P16"], _block_diag(u_["P16"], C, C)).astype(BF16)
    for s in range(n_sq):
        for (b, c) in ids:
            u_ = U[b, c]
            bd = _block_diag(u_["P16"], C, C)
            if s + 1 < n_sq:
                y = _dot(jnp.concatenate([u_["X"].astype(BF16), u_["P16"]], axis=0), bd)
                u_["X"] = u_["X"] + y[0:C]
                u_["P16"] = y[C:].astype(BF16)
            else:
                u_["X"] = u_["X"] + _dot(u_["X"].astype(BF16), bd)
    for (b, c) in ids:
        u_ = U[b, c]
        rs = slice(c * C, (c + 1) * C)
        Gc, Gr, k, q, k16, X = u_["Gc"], u_["Gr"], u_["k"], u_["q"], u_["k16"], u_["X"]
        brow = brow_ref[b, c:c + 1, :]
        Tu = (X * brow).astype(BF16)
        Tw = (X * (brow * jnp.exp(Gr))).astype(BF16)
        g_last = Gc[C - 1:C, :]
        khat = (k * _widen(jnp.exp(g_last - Gc))).astype(BF16)
        qe = (q * _widen(jnp.exp(Gc))).astype(BF16)
        v16 = v_ref[b, rs, :].astype(BF16)
        heads = []
        for hh in range(DN_HEADS):
            sl = slice(hh * DN_DK, (hh + 1) * DN_DK)
            cs = slice(hh * C, (hh + 1) * C)
            u = _dot(Tu[:, cs], v16[:, sl])
            w = _dot(Tw[:, cs], k16[:, sl])
            heads.append((u, jnp.concatenate([w.astype(BF16), qe[:, sl]], axis=0),
                          u_["attn"][:, cs].astype(BF16), khat[:, sl], jnp.exp(g_last[:, hh * C:hh * C + 1])))
        U[b, c] = heads

    state = [[st_ref[b, hh] for hh in range(DN_HEADS)] for b in range(nb)]
    streams = [(b, hh) for b in range(nb) for hh in range(DN_HEADS)]
    for c in range(chunks):
        rs = slice(c * C, (c + 1) * C)
        R, vn16 = {}, {}
        for (b, hh) in streams:
            R[b, hh] = _dot(U[b, c][hh][1], state[b][hh].astype(BF16))
        for (b, hh) in streams:
            vn16[b, hh] = (U[b, c][hh][0] - R[b, hh][0:C]).astype(BF16)
            o_ref[b, rs, hh * DN_DK:(hh + 1) * DN_DK] = R[b, hh][C:] + _dot(U[b, c][hh][2], vn16[b, hh])
        for (b, hh) in streams:
            state[b][hh] = state[b][hh] * U[b, c][hh][4] + _dot_tn(U[b, c][hh][3], vn16[b, hh])
    for b in range(nb):
        for hh in range(DN_HEADS):
            st_ref[b, hh] = state[b][hh]

    @pl.when(t == pl.num_programs(0) - 1)
    def _():
        s_ref[...] = st_ref[...]


def _dn_prompt(q, k, v, gb, gx, bx, *, batch, seq_len, tc):
    n = q.shape[0]
    nt = seq_len // tc
    chunks = tc // CHUNK
    hd = DN_HEADS * DN_DK
    gb5 = jnp.swapaxes(gb.reshape(batch, nt, chunks, CHUNK, 2 * DN_HEADS), 3, 4)
    grow = gb5[:, :, :, 0:DN_HEADS].reshape(batch, nt, chunks, CAT)
    brow = gb5[:, :, :, DN_HEADS:].reshape(batch, nt, chunks, CAT)
    r3 = lambda a: a.reshape(batch, seq_len, a.shape[-1])
    tok = lambda c: pl.BlockSpec((batch, tc, c), lambda t: (0, t, 0))
    rowspec = pl.BlockSpec((batch, None, chunks, CAT), lambda t: (0, t, 0, 0))
    o, s = pl.pallas_call(
        functools.partial(_dn_prompt_kernel, chunks=chunks),
        grid=(nt,),
        in_specs=[tok(hd), tok(hd), tok(hd), tok(CAT), tok(CAT), rowspec, rowspec],
        out_specs=[tok(hd), _const_spec((batch, DN_HEADS, DN_DK, DN_DV))],
        out_shape=[jax.ShapeDtypeStruct((batch, seq_len, hd), F32),
                   jax.ShapeDtypeStruct((batch, DN_HEADS, DN_DK, DN_DV), F32)],
        scratch_shapes=[pltpu.VMEM((batch, DN_HEADS, DN_DK, DN_DV), F32)],
        compiler_params=_cparams(("arbitrary",)),
        name="dn_prompt",
    )(r3(q), r3(k), r3(v), r3(gx), r3(bx), grow, brow)
    return o.reshape(n, hd), s


def _gla_prompt_kernel(q_ref, k_ref, v_ref, gl_ref, o_ref, s_ref, st_ref, *, chunks):
    t = pl.program_id(1)

    @pl.when(t == 0)
    def _():
        st_ref[...] = jnp.zeros_like(st_ref)

    C = CHUNK
    nsub = C // SUB
    ri = lax.broadcasted_iota(jnp.int32, (C, C), 0)
    ci = lax.broadcasted_iota(jnp.int32, (C, C), 1)
    tri_f = (ri >= ci).astype(F32)
    blk_lo = (ri // SUB) * SUB
    rloc = ri - blk_lo

    def body(c, carry):
        r0 = pl.multiple_of(c * C, C)
        Gall = _dot_f32(tri_f, gl_ref[pl.ds(r0, C), :])
        for hh in range(GLA_HEADS):
            ks = slice(hh * GLA_DK, (hh + 1) * GLA_DK)
            vs = slice(hh * GLA_DV, (hh + 1) * GLA_DV)
            q = q_ref[pl.ds(r0, C), ks]
            k = k_ref[pl.ds(r0, C), ks]
            v16 = v_ref[pl.ds(r0, C), vs].astype(BF16)
            G = Gall[:, ks]
            St = st_ref[hh]
            o = _dot_nt((q * jnp.exp(G)).astype(BF16), St.astype(BF16))
            qparts, kparts = [], []
            for I in range(1, nsub):
                lo = I * SUB
                Gr = G[lo:lo + 1, :]
                qI = q[lo:lo + SUB] * jnp.exp(G[lo:lo + SUB] - Gr)
                kI = k[0:lo] * jnp.exp(Gr - G[0:lo])
                qparts.append(jnp.concatenate(
                    ([jnp.zeros((lo, GLA_DK), F32)] if lo else []) + [qI] +
                    ([jnp.zeros((C - lo - SUB, GLA_DK), F32)] if C - lo - SUB else []), axis=0))
                kparts.append(jnp.concatenate([kI, jnp.zeros((C - lo, GLA_DK), F32)], axis=0))
            A = _dot_nt(jnp.concatenate(qparts, axis=1).astype(BF16),
                        jnp.concatenate(kparts, axis=1).astype(BF16))
            for j in range(SUB):
                ksel = jnp.concatenate(
                    [jnp.broadcast_to(k[I * SUB + j:I * SUB + j + 1], (SUB, GLA_DK)) for I in range(nsub)], axis=0)
                gsel = jnp.concatenate(
                    [jnp.broadcast_to(G[I * SUB + j:I * SUB + j + 1], (SUB, GLA_DK)) for I in range(nsub)], axis=0)
                col = jnp.sum(q * ksel * jnp.exp(jnp.minimum(G - gsel, 0.0)), axis=-1, keepdims=True)
                A = jnp.where((ci == blk_lo + j) & (rloc >= j), col, A)
            o_ref[pl.ds(r0, C), vs] = o + _dot(A.astype(BF16), v16)
            gl = G[C - 1:C, :]
            khat = (k * jnp.exp(gl - G)).astype(BF16)
            st_ref[hh] = St * jnp.exp(gl) + _dot_tn(v16, khat)
        return carry

    lax.fori_loop(0, chunks, body, 0)

    @pl.when(t == pl.num_programs(1) - 1)
    def _():
        s_ref[0] = st_ref[...]


def _gla_prompt(q, k, v, gl, *, batch, seq_len, tc):
    n = q.shape[0]
    nt = seq_len // tc
    chunks = tc // CHUNK
    rowk = pl.BlockSpec((tc, GLA_HEADS * GLA_DK), lambda b, t: (b * nt + t, 0))
    rowv = pl.BlockSpec((tc, GLA_HEADS * GLA_DV), lambda b, t: (b * nt + t, 0))
    return pl.pallas_call(
        functools.partial(_gla_prompt_kernel, chunks=chunks),
        grid=(batch, nt),
        in_specs=[rowk, rowk, rowv, rowk],
        out_specs=[rowv, pl.BlockSpec((1, GLA_HEADS, GLA_DV, GLA_DK), lambda b, t: (b, 0, 0, 0))],
        out_shape=[jax.ShapeDtypeStruct((n, GLA_HEADS * GLA_DV), F32),
                   jax.ShapeDtypeStruct((batch, GLA_HEADS, GLA_DV, GLA_DK), F32)],
        scratch_shapes=[pltpu.VMEM((GLA_HEADS, GLA_DV, GLA_DK), F32)],
        compiler_params=_cparams(("parallel", "arbitrary")),
        name="gla_prompt",
    )(q, k, v, gl)


def _alibi_slope(h):
    return 2.0 ** (-8.0 * (h + 1) / SW_HEADS)


def _swa_prompt_kernel(sink_ref, q_ref, kc_ref, vc_ref, kp_ref, vp_ref, o_ref, *, sub):
    n = pl.program_id(1)
    W = WINDOW
    ri = lax.broadcasted_iota(jnp.int32, (W, 2 * W), 0)
    ci = lax.broadcasted_iota(jnp.int32, (W, 2 * W), 1)
    rel = ri + W - ci
    valid = (rel >= 0) & (rel < W) & ((ci >= W) | (n > 0))
    relf = rel.astype(F32)
    for kh in range(SW_KV_HEADS):
        sl = slice(kh * SW_HD, (kh + 1) * SW_HD)
        K = jnp.concatenate([kp_ref[:, sl], kc_ref[:, sl]], axis=0).astype(BF16)
        V = jnp.concatenate([vp_ref[:, sl], vc_ref[:, sl]], axis=0).astype(BF16)
        for g in range(SW_GROUP):
            h = kh * SW_GROUP + g
            qh = q_ref[:, h * SW_HD:(h + 1) * SW_HD].astype(BF16)
            s = _dot_nt(qh, K) * (SW_HD ** -0.5) - _alibi_slope(h) * relf
            s = jnp.where(valid, s, -jnp.inf)
            sink = sink_ref[sub, h]
            m = jnp.maximum(jnp.max(s, axis=-1, keepdims=True), sink)
            p = jnp.exp(s - m)
            den = jnp.sum(p, axis=-1, keepdims=True) + jnp.exp(sink - m)
            p = p / den
            o_ref[:, h * SW_HD:(h + 1) * SW_HD] = _dot(p.astype(BF16), V)


def _swa_prompt(q, k, v, sinks, sub, *, batch, seq_len):
    n = q.shape[0]
    nb = seq_len // WINDOW
    cur = lambda c: pl.BlockSpec((WINDOW, c), lambda b, t: (b * nb + t, 0))
    prev = lambda c: pl.BlockSpec((WINDOW, c), lambda b, t: (b * nb + jnp.maximum(t - 1, 0), 0))
    kvc = SW_KV_HEADS * SW_HD
    return pl.pallas_call(
        functools.partial(_swa_prompt_kernel, sub=sub),
        grid=(batch, nb),
        in_specs=[pl.BlockSpec(memory_space=pltpu.SMEM), cur(SW_HEADS * SW_HD), cur(kvc), cur(kvc), prev(kvc), prev(kvc)],
        out_specs=cur(SW_HEADS * SW_HD),
        out_shape=jax.ShapeDtypeStruct((n, SW_HEADS * SW_HD), F32),
        compiler_params=_cparams(("parallel", "parallel")),
        name="swa_prompt",
    )(sinks, q, k, v, k, v)


SEQ_BLK = 8


def _state_call(kernel_fn, name, sub, stacked_states, prev_outs, other_ins, other_specs, row_out_shape, row_out_spec,
                smem_ins=()):
    nstate = len(stacked_states)

    def sspec(a):
        nd = a.ndim - 2
        return pl.BlockSpec((None, SEQ_BLK) + a.shape[2:], lambda i: (sub, i) + (0,) * nd)

    ns = stacked_states[0].shape[1]
    ins = list(smem_ins) + list(stacked_states) + list(other_ins)
    specs = ([pl.BlockSpec(memory_space=pltpu.SMEM)] * len(smem_ins) + [sspec(a) for a in stacked_states]
             + list(other_specs))
    aliases = {}
    if prev_outs is not None:
        for j, p in enumerate(prev_outs):
            aliases[len(ins)] = 1 + j
            ins.append(p)
            specs.append(pl.BlockSpec(memory_space=pl.ANY))

    def body(*refs):
        n_in = len(smem_ins) + nstate + len(other_ins)
        kernel_fn(*refs[:n_in], *refs[len(ins):])

    return pl.pallas_call(
        body,
        grid=(ns // SEQ_BLK,),
        in_specs=specs,
        out_specs=[row_out_spec] + [sspec(a) for a in stacked_states],
        out_shape=[row_out_shape] + [jax.ShapeDtypeStruct(a.shape, F32) for a in stacked_states],
        input_output_aliases=aliases,
        compiler_params=_cparams(("parallel",)),
        name=name,
    )(*ins)


def _cols_by_block(a):
    n, c = a.shape
    return jnp.swapaxes(a.reshape(n // SEQ_BLK, SEQ_BLK, c), 1, 2)


def _dn_decode_kernel(gb_ref, s_ref, qT_ref, kT_ref, v_ref, o_ref, sn_ref):
    i = pl.program_id(0)
    for j in range(SEQ_BLK):
        for hh in range(DN_HEADS):
            sl = slice(hh * DN_DK, (hh + 1) * DN_DK)
            S = s_ref[j, hh]
            kcol = kT_ref[0, sl, j:j + 1]
            qcol = qT_ref[0, sl, j:j + 1]
            vrow = v_ref[j:j + 1, sl]
            a = jnp.exp(jnp.full((1, DN_DV), gb_ref[i * SEQ_BLK + j, hh], F32))
            beta = gb_ref[i * SEQ_BLK + j, DN_HEADS + hh]
            Sa = S * a
            pred = jnp.sum(Sa * kcol, axis=0, keepdims=True)
            Sn = Sa + kcol * (beta * (vrow - pred))
            sn_ref[j, hh] = Sn
            o_ref[j:j + 1, sl] = jnp.sum(Sn * qcol, axis=0, keepdims=True)


def _dn_decode(S_all, prev, sub, q, k, v, gb):
    n = q.shape[0]
    hd = DN_HEADS * DN_DK
    colspec = pl.BlockSpec((1, hd, SEQ_BLK), lambda i: (i, 0, 0))
    rspec = pl.BlockSpec((SEQ_BLK, hd), lambda i: (i, 0))
    return _state_call(_dn_decode_kernel, "dn_decode", sub, [S_all], prev,
                       [_cols_by_block(q), _cols_by_block(k), v], [colspec, colspec, rspec],
                       jax.ShapeDtypeStruct((n, hd), F32), rspec, smem_ins=[gb])


def _gla_decode_kernel(s_ref, qT_ref, kT_ref, glT_ref, v_ref, o_ref, sn_ref):
    for j in range(SEQ_BLK):
        for hh in range(GLA_HEADS):
            ks = slice(hh * GLA_DK, (hh + 1) * GLA_DK)
            vs = slice(hh * GLA_DV, (hh + 1) * GLA_DV)
            S = s_ref[j, hh]
            kcol = kT_ref[0, ks, j:j + 1]
            qcol = qT_ref[0, ks, j:j + 1]
            acol = jnp.exp(glT_ref[0, ks, j:j + 1])
            vrow = v_ref[j:j + 1, vs]
            Sn = S * acol + kcol * vrow
            sn_ref[j, hh] = Sn
            o_ref[j:j + 1, vs] = jnp.sum(Sn * qcol, axis=0, keepdims=True)


def _gla_decode(S_all, prev, sub, q, k, v, gl):
    n = q.shape[0]
    hk = GLA_HEADS * GLA_DK
    hv = GLA_HEADS * GLA_DV
    colspec = pl.BlockSpec((1, hk, SEQ_BLK), lambda i: (i, 0, 0))
    rspec = pl.BlockSpec((SEQ_BLK, hv), lambda i: (i, 0))
    return _state_call(_gla_decode_kernel, "gla_decode", sub, [S_all], prev,
                       [_cols_by_block(q), _cols_by_block(k), _cols_by_block(gl), v],
                       [colspec, colspec, colspec, rspec],
                       jax.ShapeDtypeStruct((n, hv), F32), rspec)


def _swa_decode_kernel(sink_ref, kc_ref, vc_ref, q_ref, kn_ref, vn_ref, o_ref, ko_ref, vo_ref, *, sub):
    W = WINDOW
    wi = lax.broadcasted_iota(jnp.int32, (SW_GROUP, W), 1)
    relf = (W - wi).astype(F32)
    valid = wi >= 1
    gi = lax.broadcasted_iota(jnp.int32, (SW_GROUP, 1), 0)
    for j in range(SEQ_BLK):
        ko_ref[j, 0:W - 1, :] = kc_ref[j, 1:W, :]
        ko_ref[j, W - 1:W, :] = kn_ref[j:j + 1, :]
        vo_ref[j, 0:W - 1, :] = vc_ref[j, 1:W, :]
        vo_ref[j, W - 1:W, :] = vn_ref[j:j + 1, :]
        for kh in range(SW_KV_HEADS):
            sl = slice(kh * SW_HD, (kh + 1) * SW_HD)
            Kc = kc_ref[j, :, sl].astype(BF16)
            Vc = vc_ref[j, :, sl].astype(BF16)
            knew = kn_ref[j:j + 1, sl]
            vnew = vn_ref[j:j + 1, sl]
            qg = jnp.concatenate(
                [q_ref[j:j + 1, (kh * SW_GROUP + g) * SW_HD:(kh * SW_GROUP + g + 1) * SW_HD] for g in range(SW_GROUP)],
                axis=0)
            slope = jnp.zeros((SW_GROUP, 1), F32)
            sink = jnp.zeros((SW_GROUP, 1), F32)
            for g in range(SW_GROUP):
                slope = jnp.where(gi == g, _alibi_slope(kh * SW_GROUP + g), slope)
                sink = jnp.where(gi == g, sink_ref[sub, kh * SW_GROUP + g], sink)
            s = _dot_nt(qg.astype(BF16), Kc) * (SW_HD ** -0.5) - slope * relf
            s = jnp.where(valid, s, -jnp.inf)
            s_new = jnp.sum(qg * knew, axis=-1, keepdims=True) * (SW_HD ** -0.5)
            m = jnp.maximum(jnp.maximum(jnp.max(s, axis=-1, keepdims=True), s_new), sink)
            p = jnp.exp(s - m)
            p_new = jnp.exp(s_new - m)
            den = jnp.sum(p, axis=-1, keepdims=True) + p_new + jnp.exp(sink - m)
            og = _dot((p / den).astype(BF16), Vc) + (p_new / den) * vnew
            for g in range(SW_GROUP):
                h = kh * SW_GROUP + g
                o_ref[j:j + 1, h * SW_HD:(h + 1) * SW_HD] = og[g:g + 1]


def _swa_decode(kc_all, vc_all, prev, sub, q, kn, vn, sinks):
    n = q.shape[0]
    kvc = SW_KV_HEADS * SW_HD
    rq = pl.BlockSpec((SEQ_BLK, SW_HEADS * SW_HD), lambda i: (i, 0))
    rkv = pl.BlockSpec((SEQ_BLK, kvc), lambda i: (i, 0))
    return _state_call(functools.partial(_swa_decode_kernel, sub=sub), "swa_decode", sub, [kc_all, vc_all], prev,
                       [q, kn, vn], [rq, rkv, rkv],
                       jax.ShapeDtypeStruct((n, SW_HEADS * SW_HD), F32), rq, smem_ins=[sinks])


def _pad_last(a, width):
    return jnp.pad(a, [(0, 0)] * (a.ndim - 1) + [(0, width - a.shape[-1])])


def kernel(x_prompt, x_sample, state_dn, state_dn_conv, cache_swa_k, cache_swa_v, state_gla,
           norm_mix, norm_mlp, norm_final,
           even_w_in, dn_conv_w, dn_a_log, dn_dt_bias, dn_norm, sw_sinks, even_w_out,
           gla_w_in, gla_w_gate_up, gla_b_gate, gla_norm, gla_w_out,
           mlp_w_up, mlp_w_down):
    B, L, D = x_prompt.shape
    NS = x_sample.shape[0]
    depth = norm_mix.shape[0]
    n_even = even_w_in.shape[0]
    n_odd = gla_w_in.shape[0]
    kvc = SW_KV_HEADS * SW_HD
    xp = x_prompt.reshape(B * L, D)
    xs = x_sample.reshape(NS, D)
    tm_p = 512
    tc_dn = 256
    tc_gla = 512
    no_hist = jnp.zeros((SUBLANES, LANES), F32)

    nmix = norm_mix.reshape(depth, 1, D)
    nmlp = norm_mlp.reshape(depth, 1, D)
    nfin = norm_final.reshape(1, D)
    wu = mlp_w_up.astype(BF16)
    wd = mlp_w_down.astype(BF16)
    ab0 = EV_QKV + 512
    sw0 = ab0 + 2 * DN_HEADS
    w_ab = even_w_in[:, :, ab0:sw0]
    ev_w = jnp.concatenate(
        [even_w_in[:, :, :ab0], even_w_in[:, :, sw0:], _pad_last(w_ab, LANES),
         jnp.repeat(w_ab[:, :, 0:DN_HEADS], CHUNK, axis=2), jnp.repeat(w_ab[:, :, DN_HEADS:], CHUNK, axis=2)],
        axis=2).astype(BF16)
    alog = _pad_last(dn_a_log, LANES).reshape(n_even, 1, LANES)
    dtb = _pad_last(dn_dt_bias, LANES).reshape(n_even, 1, LANES)
    alogx = jnp.repeat(dn_a_log, CHUNK, axis=1).reshape(n_even, 1, CAT)
    dtbx = jnp.repeat(dn_dt_bias, CHUNK, axis=1).reshape(n_even, 1, CAT)
    dn_gn = dn_norm.reshape(n_even, 1, DN_DV)
    ev_wo = even_w_out.astype(BF16)
    sinks = sw_sinks.astype(F32)
    od_w = jnp.concatenate([gla_w_in[:, :, :OD_G], _pad_last(gla_w_in[:, :, OD_G:], LANES)], axis=2).astype(BF16)
    od_wg = jnp.pad(gla_w_gate_up, ((0, 0), (0, LANES - GLA_RANK), (0, 0))).astype(BF16)
    od_bg = gla_b_gate.reshape(n_odd, 1, -1)
    gla_gn = gla_norm.reshape(n_odd, 1, GLA_DV)
    od_wo = gla_w_out.astype(BF16)
    kc_all = cache_swa_k.reshape(n_even, NS, WINDOW, kvc)
    vc_all = cache_swa_v.reshape(n_even, NS, WINDOW, kvc)

    p_dn, p_conv, p_k, p_v, p_gla, s_conv = [], [], [], [], [], []
    s_dn = s_kv = s_gla = None
    for layer in range(depth):
        nf = nfin if layer == depth - 1 else None
        if layer % 2 == 0:
            e = layer // 2
            q, k, v, z, gb, gx, bx, qsw, ksw, vsw, tail = _even_in(
                xp, nmix, layer, ev_w, dn_conv_w, alog, dtb, alogx, dtbx, e, no_hist,
                tm=tm_p, seq_len=L, per_token_hist=False)
            o_dn, S = _dn_prompt(q, k, v, gb, gx, bx, batch=B, seq_len=L, tc=tc_dn)
            o_sw = _swa_prompt(qsw, ksw, vsw, sinks, e, batch=B, seq_len=L)
            xp = _post_mlp(xp, (o_dn, z, o_sw), dn_gn, ev_wo, nmlp, wu, wd, nf, layer, e, tm=tm_p, even=True)
            p_dn.append(S)
            tiles = L // tm_p
            p_conv.append(tail.reshape(B, tiles, SUBLANES, EV_QKV)[:, tiles - 1, SUBLANES - (DN_CONV - 1):])
            p_k.append(ksw.reshape(B, L, SW_KV_HEADS, SW_HD)[:, L - WINDOW:])
            p_v.append(vsw.reshape(B, L, SW_KV_HEADS, SW_HD)[:, L - WINDOW:])
            hist = jnp.swapaxes(state_dn_conv[e], 0, 1)
            q, k, v, z, gb, gx, bx, qsw, ksw, vsw, upre = _even_in(
                xs, nmix, layer, ev_w, dn_conv_w, alog, dtb, alogx, dtbx, e, hist,
                tm=NS, seq_len=1, per_token_hist=True)
            o_dn, s_dn_new = _dn_decode(state_dn, None if s_dn is None else [s_dn], e, q, k, v, gb)
            s_dn = s_dn_new
            o_sw, nk, nv = _swa_decode(kc_all, vc_all, s_kv, e, qsw, ksw, vsw, sinks)
            s_kv = [nk, nv]
            xs = _post_mlp(xs, (o_dn, z, o_sw), dn_gn, ev_wo, nmlp, wu, wd, nf, layer, e, tm=NS, even=True)
            s_conv.append(jnp.concatenate([state_dn_conv[e][:, 1:], upre[:, None, :]], axis=1))
        else:
            o_i = layer // 2
            q, k, v, r, gl = _odd_in(xp, nmix, layer, od_w, od_wg, od_bg, o_i, tm=tm_p)
            o, St = _gla_prompt(q, k, v, gl, batch=B, seq_len=L, tc=tc_gla)
            xp = _post_mlp(xp, (o, r), gla_gn, od_wo, nmlp, wu, wd, nf, layer, o_i, tm=tm_p, even=False)
            p_gla.append(jnp.swapaxes(St, 2, 3))
            q, k, v, r, gl = _odd_in(xs, nmix, layer, od_w, od_wg, od_bg, o_i, tm=NS)
            o, s_gla_new = _gla_decode(state_gla, None if s_gla is None else [s_gla], o_i, q, k, v, gl)
            s_gla = s_gla_new
            xs = _post_mlp(xs, (o, r), gla_gn, od_wo, nmlp, wu, wd, nf, layer, o_i, tm=NS, even=False)

    cache_shape = (n_even, NS, WINDOW, SW_KV_HEADS, SW_HD)
    return (xp.reshape(B, L, D), xs.reshape(NS, 1, D),
            jnp.stack(p_dn), jnp.stack(p_conv), jnp.stack(p_k), jnp.stack(p_v), jnp.stack(p_gla),
            s_dn, jnp.stack(s_conv), s_kv[0].reshape(cache_shape), s_kv[1].reshape(cache_shape), s_gla)
```

```python
import functools

import jax
import jax.numpy as jnp
from jax import lax
from jax.experimental import pallas as pl
from jax.experimental.pallas import tpu as pltpu

F32 = jnp.float32
BF16 = jnp.bfloat16
EPS = 1e-6

DN_HEADS = 4
DN_DK = 128
DN_DV = 128
DN_CONV = 4
SW_HEADS = 8
SW_KV_HEADS = 2
SW_GROUP = SW_HEADS // SW_KV_HEADS
SW_HD = 64
WINDOW = 128
GLA_HEADS = 4
GLA_DK = 128
GLA_DV = 256
GLA_RANK = 16
GLA_TAU = 16.0

LANES = 128
SUBLANES = 8
CHUNK = 64
LOG2E = 1.4426950408889634
VMEM_LIMIT = 56 * 1024 * 1024
CAT = DN_HEADS * CHUNK


def _dot(a, b):
    return jnp.dot(a, b, preferred_element_type=F32)


def _dot_nt(a, b):
    return lax.dot_general(a, b, (((1,), (1,)), ((), ())), preferred_element_type=F32)


def _dot_tn(a, b):
    return lax.dot_general(a, b, (((0,), (0,)), ((), ())), preferred_element_type=F32)


def _dot_f32(a, b):
    return jnp.dot(a, b, preferred_element_type=F32, precision=lax.Precision.HIGHEST)


def _rms(x, g):
    return x * lax.rsqrt(jnp.mean(x * x, axis=-1, keepdims=True) + EPS) * g


def _silu(x):
    return x * jax.nn.sigmoid(x)


def _softplus(x):
    return jnp.maximum(x, 0.0) + jnp.log1p(jnp.exp(-jnp.abs(x)))


def _log_sigmoid(x):
    return jnp.minimum(x, 0.0) - jnp.log1p(jnp.exp(-jnp.abs(x)))


def _cparams(sem):
    return pltpu.CompilerParams(dimension_semantics=sem, vmem_limit_bytes=VMEM_LIMIT)


def _const_spec(shape):
    nd = len(shape)
    return pl.BlockSpec(shape, lambda *_: (0,) * nd)


def _layer_spec(a, layer):
    nd = a.ndim - 1
    return pl.BlockSpec((None,) + a.shape[1:], lambda *_: (layer,) + (0,) * nd)


def _seg_cumsum(x, seg, axis):
    pos = lax.broadcasted_iota(jnp.int32, x.shape, axis) & (seg - 1)
    s = 1
    while s < seg:
        x = x + jnp.where(pos >= s, pltpu.roll(x, s, axis), 0.0)
        s *= 2
    return x


EV_QKV = 3 * DN_HEADS * DN_DK
EV_Z = EV_QKV
EV_QSW = EV_Z + 512
EV_KSW = EV_QSW + 512
EV_VSW = EV_KSW + 128
EV_AB = EV_VSW + 128
EV_AX = EV_AB + LANES
EV_BX = EV_AX + CAT
EV_COLS = EV_BX + CAT


def _even_in_kernel(x_ref, nw_ref, w_ref, cw_ref, alog_ref, dtb_ref, alogx_ref, dtbx_ref, hist_ref,
                    q_ref, k_ref, v_ref, z_ref, gb_ref, gx_ref, bx_ref, qsw_ref, ksw_ref, vsw_ref, upre_ref,
                    carry_ref, *, tiles_per_seq, per_token_hist):
    i = pl.program_id(0)
    tm = x_ref.shape[0]
    h = _rms(x_ref[...], nw_ref[...]).astype(BF16)

    if not per_token_hist:
        @pl.when(i % tiles_per_seq == 0)
        def _():
            carry_ref[0:SUBLANES, :] = jnp.zeros((SUBLANES, EV_QKV), F32)

    def conv_section(u, c0):
        cw = cw_ref[:, c0:c0 + 512]
        if per_token_hist:
            upre_ref[:, c0:c0 + 512] = u
            out = u * cw[3:4]
            for t in range(DN_CONV - 1):
                out = out + hist_ref[t, :, c0:c0 + 512] * cw[t:t + 1]
            return out
        carry_ref[SUBLANES:SUBLANES + tm, c0:c0 + 512] = u
        out = u * cw[3:4]
        for s in range(1, DN_CONV):
            out = out + carry_ref[SUBLANES - s:SUBLANES - s + tm, c0:c0 + 512] * cw[3 - s:4 - s]
        carry_ref[0:SUBLANES, c0:c0 + 512] = u[tm - SUBLANES:tm]
        upre_ref[:, c0:c0 + 512] = u[tm - SUBLANES:tm]
        return out

    def l2n(y):
        outs = []
        for hh in range(DN_HEADS):
            yh = y[:, hh * DN_DK:(hh + 1) * DN_DK]
            outs.append(yh * lax.rsqrt(jnp.sum(yh * yh, axis=-1, keepdims=True) + EPS))
        return jnp.concatenate(outs, axis=1)

    q_ref[...] = l2n(_silu(conv_section(_dot(h, w_ref[:, 0:512]), 0))) * (DN_DK ** -0.5)
    k_ref[...] = l2n(_silu(conv_section(_dot(h, w_ref[:, 512:1024]), 512)))
    v_ref[...] = _silu(conv_section(_dot(h, w_ref[:, 1024:1536]), 1024))
    z_ref[...] = _dot(h, w_ref[:, EV_Z:EV_Z + 512])
    qsw_ref[...] = _dot(h, w_ref[:, EV_QSW:EV_QSW + 512])
    ksw_ref[...] = _dot(h, w_ref[:, EV_KSW:EV_KSW + 128])
    vsw_ref[...] = _dot(h, w_ref[:, EV_VSW:EV_VSW + 128])
    ab = _dot(h, w_ref[:, EV_AB:EV_AB + LANES])
    g = -jnp.exp(alog_ref[...]) * _softplus(ab + dtb_ref[...])
    lane = lax.broadcasted_iota(jnp.int32, ab.shape, 1)
    gb = jnp.where(lane < DN_HEADS, g, jax.nn.sigmoid(ab))
    gb_ref[...] = gb[:, 0:2 * DN_HEADS]
    ax = _dot(h, w_ref[:, EV_AX:EV_AX + CAT])
    gx_ref[...] = -jnp.exp(alogx_ref[...]) * _softplus(ax + dtbx_ref[...])
    bx_ref[...] = jax.nn.sigmoid(_dot(h, w_ref[:, EV_BX:EV_BX + CAT]))


def _even_in(x, nw, layer, w, cw, alog, dtb, alogx, dtbx, sub, hist, *, tm, seq_len, per_token_hist):
    n, d = x.shape
    nt = n // tm
    tiles_per_seq = max(seq_len // tm, 1)
    row = lambda c: pl.BlockSpec((tm, c), lambda i: (i, 0))
    cols = (512, 512, 512, 512, 2 * DN_HEADS, CAT, CAT, 512, 128, 128)
    out_shapes = [jax.ShapeDtypeStruct((n, c), F32) for c in cols]
    out_specs = [row(c) for c in cols]
    if per_token_hist:
        hist_spec = pl.BlockSpec((DN_CONV - 1, tm, EV_QKV), lambda i: (0, i, 0))
        out_shapes.append(jax.ShapeDtypeStruct((n, EV_QKV), F32))
        out_specs.append(row(EV_QKV))
    else:
        hist_spec = _const_spec(hist.shape)
        out_shapes.append(jax.ShapeDtypeStruct((nt, SUBLANES, EV_QKV), F32))
        out_specs.append(pl.BlockSpec((None, SUBLANES, EV_QKV), lambda i: (i, 0, 0)))
    return pl.pallas_call(
        functools.partial(_even_in_kernel, tiles_per_seq=tiles_per_seq, per_token_hist=per_token_hist),
        grid=(nt,),
        in_specs=([row(d), _layer_spec(nw, layer)] + [_layer_spec(a, sub) for a in (w, cw, alog, dtb, alogx, dtbx)]
                  + [hist_spec]),
        out_specs=out_specs,
        out_shape=out_shapes,
        scratch_shapes=[pltpu.VMEM((SUBLANES + (0 if per_token_hist else tm), EV_QKV), F32)],
        compiler_params=_cparams(("arbitrary",)),
        name="even_in",
    )(x, nw, w, cw, alog, dtb, alogx, dtbx, hist)


OD_K = 512
OD_V = 1024
OD_R = 2048
OD_G = 3072
OD_COLS = OD_G + LANES


def _odd_in_kernel(x_ref, nw_ref, w_ref, wg_ref, bg_ref, q_ref, k_ref, v_ref, r_ref, gl_ref):
    h = _rms(x_ref[...], nw_ref[...]).astype(BF16)
    q_ref[...] = _dot(h, w_ref[:, 0:512]) * (GLA_DK ** -0.5)
    k_ref[...] = _dot(h, w_ref[:, OD_K:OD_K + 512])
    for c in range(2):
        v_ref[:, c * 512:(c + 1) * 512] = _dot(h, w_ref[:, OD_V + c * 512:OD_V + (c + 1) * 512])
        r_ref[:, c * 512:(c + 1) * 512] = _dot(h, w_ref[:, OD_R + c * 512:OD_R + (c + 1) * 512])
    gd = _dot(h, w_ref[:, OD_G:OD_G + LANES])
    x = _dot(gd.astype(BF16), wg_ref[...]) + bg_ref[...]
    gl_ref[...] = _log_sigmoid(x) * (1.0 / GLA_TAU)


def _odd_in(x, nw, layer, w, wg, bg, sub, *, tm):
    n, d = x.shape
    row = lambda c: pl.BlockSpec((tm, c), lambda i: (i, 0))
    cols = (512, 512, 1024, 1024, 512)
    return pl.pallas_call(
        _odd_in_kernel,
        grid=(n // tm,),
        in_specs=[row(d), _layer_spec(nw, layer)] + [_layer_spec(a, sub) for a in (w, wg, bg)],
        out_specs=[row(c) for c in cols],
        out_shape=[jax.ShapeDtypeStruct((n, c), F32) for c in cols],
        compiler_params=_cparams(("parallel",)),
        name="odd_in",
    )(x, nw, w, wg, bg)


FF_CHUNK = 512


def _post_mlp_kernel(*refs, even, final):
    if even:
        x_ref, odn_ref, z_ref, osw_ref, gn_ref, wo_ref, nm_ref, wu_ref, wd_ref = refs[:9]
        rest = refs[9:]
    else:
        x_ref, o_ref, r_ref, gn_ref, wo_ref, nm_ref, wu_ref, wd_ref = refs[:8]
        rest = refs[8:]
    if final:
        nf_ref, y_ref = rest
    else:
        (y_ref,) = rest

    x = x_ref[...]
    gn = gn_ref[...]
    if even:
        parts = []
        for hh in range(DN_HEADS):
            sl = slice(hh * DN_DV, (hh + 1) * DN_DV)
            parts.append(_rms(odn_ref[:, sl], gn) * _silu(z_ref[:, sl]))
        a = jnp.concatenate(parts, axis=1).astype(BF16)
        nv = DN_HEADS * DN_DV
        mix = _dot(a, wo_ref[0:nv, :]) + _dot(osw_ref[...].astype(BF16), wo_ref[nv:, :])
    else:
        parts = []
        for hh in range(GLA_HEADS):
            sl = slice(hh * GLA_DV, (hh + 1) * GLA_DV)
            parts.append(_rms(o_ref[:, sl], gn) * _silu(r_ref[:, sl]))
        mix = _dot(jnp.concatenate(parts, axis=1).astype(BF16), wo_ref[...])
    x1 = x + mix
    h2 = _rms(x1, nm_ref[...]).astype(BF16)
    d_ff = wu_ref.shape[1]
    acc = x1
    for c in range(d_ff // FF_CHUNK):
        sl = slice(c * FF_CHUNK, (c + 1) * FF_CHUNK)
        up = jnp.maximum(_dot(h2, wu_ref[:, sl]), 0.0)
        acc = acc + _dot((up * up).astype(BF16), wd_ref[sl, :])
    if final:
        acc = _rms(acc, nf_ref[...])
    y_ref[...] = acc


def _post_mlp(x, mixer_outs, gn, wo, nm, wu, wd, nf, layer, sub, *, tm, even):
    n, d = x.shape
    row = lambda c: pl.BlockSpec((tm, c), lambda i: (i, 0))
    final = nf is not None
    ins = [x, *mixer_outs, gn, wo, nm, wu, wd]
    specs = ([row(d)] + [row(a.shape[1]) for a in mixer_outs]
             + [_layer_spec(gn, sub), _layer_spec(wo, sub)] + [_layer_spec(a, layer) for a in (nm, wu, wd)])
    if final:
        ins.append(nf)
        specs.append(_const_spec(nf.shape))
    return pl.pallas_call(
        functools.partial(_post_mlp_kernel, even=even, final=final),
        grid=(n // tm,),
        in_specs=specs,
        out_specs=row(d),
        out_shape=jax.ShapeDtypeStruct((n, d), F32),
        compiler_params=_cparams(("parallel",)),
        name="post_mlp_even" if even else "post_mlp_odd",
    )(*ins)


def _block_diag(p16, row_blk, col_blk):
    R, Cc = p16.shape
    t = jnp.concatenate([p16] * DN_HEADS, axis=0)
    ri = lax.broadcasted_iota(jnp.int32, t.shape, 0) // row_blk
    ci = lax.broadcasted_iota(jnp.int32, t.shape, 1) // col_blk
    return jnp.where(ri == ci, t, jnp.zeros_like(t))


def _widen(x):
    lane = lax.broadcasted_iota(jnp.int32, (x.shape[0], LANES), 1)
    outs = []
    for t in range(CAT // LANES):
        xt = x[:, t * LANES:(t + 1) * LANES]
        xr = pltpu.roll(xt, CHUNK, 1)
        outs.append(jnp.where(lane < CHUNK, xt, xr))
        outs.append(jnp.where(lane < CHUNK, xr, xt))
    return jnp.concatenate(outs, axis=1)


def _dn_prompt_kernel(q_ref, k_ref, v_ref, gx_ref, bx_ref, grow_ref, brow_ref, o_ref, s_ref, st_ref, *, chunks):
    t = pl.program_id(0)
    nb = q_ref.shape[0]

    @pl.when(t == 0)
    def _():
        st_ref[...] = jnp.zeros_like(st_ref)

    C = CHUNK
    ri = lax.broadcasted_iota(jnp.int32, (C, CAT), 0)
    cj = lax.broadcasted_iota(jnp.int32, (C, CAT), 1) & (C - 1)
    tri_incl = ri >= cj
    tri_strict = ri > cj
    eye = (ri == cj).astype(F32)
    n_sq = max((C - 1).bit_length() - 1, 0)

    Gcols = [_seg_cumsum(gx_ref[b], C, 0) for b in range(nb)]
    Grows = [_seg_cumsum(grow_ref[b], C, 1) for b in range(nb)]
    ids = [(b, c) for c in range(chunks) for b in range(nb)]
    U = {}
    for (b, c) in ids:
        rs = slice(c * C, (c + 1) * C)
        Gc = Gcols[b][rs]
        Gr = Grows[b][c:c + 1]
        decay = jnp.where(tri_incl, jnp.exp(jnp.where(tri_incl, Gc - Gr, 0.0)), 0.0)
        k = k_ref[b, rs, :]
        q = q_ref[b, rs, :]
        k16 = k.astype(BF16)
        kq = _dot_nt(jnp.concatenate([k16, q.astype(BF16)], axis=0), _block_diag(k16, C, DN_DK))
        M = jnp.where(tri_strict, kq[0:C] * bx_ref[b, rs, :] * decay, 0.0)
        U[b, c] = dict(Gc=Gc, Gr=Gr, k=k, q=q, k16=k16, attn=kq[C:] * decay, X=eye - M, P16=M.astype(BF16))
    for (b, c) in ids:
        u_ = U[b, c]
        u_["P16"] = _dot(u_["P16"], _block_diag(u_["P16"], C, C)).astype(BF16)
    for s in range(n_sq):
        for (b, c) in ids:
            u_ = U[b, c]
            bd = _block_diag(u_["P16"], C, C)
            if s + 1 < n_sq:
                y = _dot(jnp.concatenate([u_["X"].astype(BF16), u_["P16"]], axis=0), bd)
                u_["X"] = u_["X"] + y[0:C]
                u_["P16"] = y[C:].astype(BF16)
            else:
                u_["X"] = u_["X"] + _dot(u_["X"].astype(BF16), bd)
    for (b, c) in ids:
        u_ = U[b, c]
        rs = slice(c * C, (c + 1) * C)
        Gc, Gr, k, q, k16, X = u_["Gc"], u_["Gr"], u_["k"], u_["q"], u_["k16"], u_["X"]
        brow = brow_ref[b, c:c + 1, :]
        Tu = (X * brow).astype(BF16)
        Tw = (X * (brow * jnp.exp(Gr))).astype(BF16)
        g_last = Gc[C - 1:C, :]
        khat = (k * _widen(jnp.exp(g_last - Gc))).astype(BF16)
        qe = (q * _widen(jnp.exp(Gc))).astype(BF16)
        v16 = v_ref[b, rs, :].astype(BF16)
        heads = []
        for hh in range(DN_HEADS):
            sl = slice(hh * DN_DK, (hh + 1) * DN_DK)
            cs = slice(hh * C, (hh + 1) * C)
            u = _dot(Tu[:, cs], v16[:, sl])
            w = _dot(Tw[:, cs], k16[:, sl])
            heads.append((u, jnp.concatenate([w.astype(BF16), qe[:, sl]], axis=0),
                          u_["attn"][:, cs].astype(BF16), khat[:, sl], jnp.exp(g_last[:, hh * C:hh * C + 1])))
        U[b, c] = heads

    state = [[st_ref[b, hh] for hh in range(DN_HEADS)] for b in range(nb)]
    streams = [(b, hh) for b in range(nb) for hh in range(DN_HEADS)]
    for c in range(chunks):
        rs = slice(c * C, (c + 1) * C)
        R, vn16 = {}, {}
        for (b, hh) in streams:
            R[b, hh] = _dot(U[b, c][hh][1], state[b][hh].astype(BF16))
        for (b, hh) in streams:
            vn16[b, hh] = (U[b, c][hh][0] - R[b, hh][0:C]).astype(BF16)
            o_ref[b, rs, hh * DN_DK:(hh + 1) * DN_DK] = R[b, hh][C:] + _dot(U[b, c][hh][2], vn16[b, hh])
        for (b, hh) in streams:
            state[b][hh] = state[b][hh] * U[b, c][hh][4] + _dot_tn(U[b, c][hh][3], vn16[b, hh])
    for b in range(nb):
        for hh in range(DN_HEADS):
            st_ref[b, hh] = state[b][hh]

    @pl.when(t == pl.num_programs(0) - 1)
    def _():
        s_ref[...] = st_ref[...]


def _dn_prompt(q, k, v, gb, gx, bx, *, batch, seq_len, tc):
    n = q.shape[0]
    nt = seq_len // tc
    chunks = tc // CHUNK
    hd = DN_HEADS * DN_DK
    gb5 = jnp.swapaxes(gb.reshape(batch, nt, chunks, CHUNK, 2 * DN_HEADS), 3, 4)
    grow = gb5[:, :, :, 0:DN_HEADS].reshape(batch, nt, chunks, CAT)
    brow = gb5[:, :, :, DN_HEADS:].reshape(batch, nt, chunks, CAT)
    r3 = lambda a: a.reshape(batch, seq_len, a.shape[-1])
    tok = lambda c: pl.BlockSpec((batch, tc, c), lambda t: (0, t, 0))
    rowspec = pl.BlockSpec((batch, None, chunks, CAT), lambda t: (0, t, 0, 0))
    o, s = pl.pallas_call(
        functools.partial(_dn_prompt_kernel, chunks=chunks),
        grid=(nt,),
        in_specs=[tok(hd), tok(hd), tok(hd), tok(CAT), tok(CAT), rowspec, rowspec],
        out_specs=[tok(hd), _const_spec((batch, DN_HEADS, DN_DK, DN_DV))],
        out_shape=[jax.ShapeDtypeStruct((batch, seq_len, hd), F32),
                   jax.ShapeDtypeStruct((batch, DN_HEADS, DN_DK, DN_DV), F32)],
        scratch_shapes=[pltpu.VMEM((batch, DN_HEADS, DN_DK, DN_DV), F32)],
        compiler_params=_cparams(("arbitrary",)),
        name="dn_prompt",
    )(r3(q), r3(k), r3(v), r3(gx), r3(bx), grow, brow)
    return o.reshape(n, hd), s


def _row_in_tile(x, r):
    R, Ln = x.shape
    x3 = x.reshape(R // SUBLANES, SUBLANES, Ln)
    return jnp.broadcast_to(x3[:, r:r + 1, :], x3.shape).reshape(R, Ln)


def _gla_intra(q, k, G2, sub_i, tile_lower):
    C = q.shape[0]
    nt = C // SUBLANES
    zrow = lambda n: jnp.zeros((n, GLA_DK), BF16)
    qs = (q * jnp.exp2(G2 - _row_in_tile(G2, 0))).astype(BF16)
    q16 = q.astype(BF16)
    q1, k1 = [], []
    for I in range(1, nt):
        lo = I * SUBLANES
        q1.append(jnp.concatenate([zrow(lo), qs[lo:lo + SUBLANES]] + ([zrow(C - lo - SUBLANES)] if C - lo - SUBLANES else []),
                                  axis=0))
        k1.append(jnp.concatenate([(k[0:lo] * jnp.exp2(G2[lo:lo + 1] - G2[0:lo])).astype(BF16), zrow(C - lo)], axis=0))
    a1 = _dot_nt(jnp.concatenate(q1, axis=1), jnp.concatenate(k1, axis=1))
    q2, k2 = [], []
    for I in range(SUBLANES):
        q2.append(jnp.where(sub_i == I, q16, jnp.zeros_like(q16)))
        k2.append((k * jnp.exp2(jnp.minimum(_row_in_tile(G2, I) - G2, 0.0))).astype(BF16))
    a2 = _dot_nt(jnp.concatenate(q2, axis=1), jnp.concatenate(k2, axis=1))
    return a1 + jnp.where(tile_lower, a2, 0.0)


def _gla_prompt_kernel(q_ref, k_ref, v_ref, gl_ref, o_ref, s_ref, st_ref, *, chunks):
    t = pl.program_id(0)
    nb = q_ref.shape[0]

    @pl.when(t == 0)
    def _():
        st_ref[...] = jnp.zeros_like(st_ref)

    C = CHUNK
    sub_i = lax.broadcasted_iota(jnp.int32, (C, GLA_DK), 0) & (SUBLANES - 1)
    ri = lax.broadcasted_iota(jnp.int32, (C, C), 0)
    ci = lax.broadcasted_iota(jnp.int32, (C, C), 1)
    tile_lower = ((ri // SUBLANES) == (ci // SUBLANES)) & (ci <= ri)

    Gs = [_seg_cumsum(gl_ref[b] * LOG2E, C, 0) for b in range(nb)]
    ids = [(b, c, hh) for c in range(chunks) for b in range(nb) for hh in range(GLA_HEADS)]
    U = {}
    for (b, c, hh) in ids:
        rs = slice(c * C, (c + 1) * C)
        ks = slice(hh * GLA_DK, (hh + 1) * GLA_DK)
        q = q_ref[b, rs, ks]
        k = k_ref[b, rs, ks]
        G = Gs[b][rs, ks]
        A = _gla_intra(q, k, G, sub_i, tile_lower)
        gl = G[C - 1:C, :]
        U[b, c, hh] = dict(A=A.astype(BF16), qe=(q * jnp.exp2(G)).astype(BF16),
                           khat=(k * jnp.exp2(gl - G)).astype(BF16), e_last=jnp.exp2(gl))
    for (b, c, hh) in ids:
        u_ = U[b, c, hh]
        v16 = v_ref[b, c * C:(c + 1) * C, hh * GLA_DV:(hh + 1) * GLA_DV].astype(BF16)
        u_["o"] = _dot(u_["A"], v16)
        u_["B"] = _dot_tn(v16, u_["khat"])
    for b in range(nb):
        for hh in range(GLA_HEADS):
            St = st_ref[b, hh]
            for c in range(chunks):
                U[b, c, hh]["St"] = St.astype(BF16)
                St = St * U[b, c, hh]["e_last"] + U[b, c, hh]["B"]
            st_ref[b, hh] = St
    for (b, c, hh) in ids:
        u_ = U[b, c, hh]
        o_ref[b, c * C:(c + 1) * C, hh * GLA_DV:(hh + 1) * GLA_DV] = u_["o"] + _dot_nt(u_["qe"], u_["St"])

    @pl.when(t == pl.num_programs(0) - 1)
    def _():
        s_ref[...] = st_ref[...]


def _gla_prompt(q, k, v, gl, *, batch, seq_len, tc):
    n = q.shape[0]
    nt = seq_len // tc
    chunks = tc // CHUNK
    hk = GLA_HEADS * GLA_DK
    hv = GLA_HEADS * GLA_DV
    r3 = lambda a: a.reshape(batch, seq_len, a.shape[-1])
    tok = lambda c: pl.BlockSpec((batch, tc, c), lambda t: (0, t, 0))
    o, s = pl.pallas_call(
        functools.partial(_gla_prompt_kernel, chunks=chunks),
        grid=(nt,),
        in_specs=[tok(hk), tok(hk), tok(hv), tok(hk)],
        out_specs=[tok(hv), _const_spec((batch, GLA_HEADS, GLA_DV, GLA_DK))],
        out_shape=[jax.ShapeDtypeStruct((batch, seq_len, hv), F32),
                   jax.ShapeDtypeStruct((batch, GLA_HEADS, GLA_DV, GLA_DK), F32)],
        scratch_shapes=[pltpu.VMEM((batch, GLA_HEADS, GLA_DV, GLA_DK), F32)],
        compiler_params=_cparams(("arbitrary",)),
        name="gla_prompt",
    )(r3(q), r3(k), r3(v), r3(gl))
    return o.reshape(n, hv), s


def _alibi_slope(h):
    return 2.0 ** (-8.0 * (h + 1) / SW_HEADS)


def _swa_prompt_kernel(sink_ref, q_ref, kc_ref, vc_ref, kp_ref, vp_ref, o_ref, *, sub):
    n = pl.program_id(1)
    W = WINDOW
    GW = SW_GROUP * W
    ri = lax.broadcasted_iota(jnp.int32, (GW, 2 * W), 0)
    ci = lax.broadcasted_iota(jnp.int32, (GW, 2 * W), 1)
    rel = (ri & (W - 1)) + W - ci
    valid = (rel >= 0) & (rel < W) & ((ci >= W) | (n > 0))
    relf = rel.astype(F32)
    gcol = lax.broadcasted_iota(jnp.int32, (GW, 1), 0) // W
    ones = jnp.ones((2 * W, SW_HD), BF16)
    scores, vext, sinks = [], [], []
    for kh in range(SW_KV_HEADS):
        sl = slice(kh * SW_HD, (kh + 1) * SW_HD)
        K = jnp.concatenate([kp_ref[:, sl], kc_ref[:, sl]], axis=0).astype(BF16)
        V = jnp.concatenate([vp_ref[:, sl], vc_ref[:, sl]], axis=0).astype(BF16)
        vext.append(jnp.concatenate([V, ones], axis=1))
        qg = jnp.concatenate([q_ref[:, (kh * SW_GROUP + g) * SW_HD:(kh * SW_GROUP + g + 1) * SW_HD]
                              for g in range(SW_GROUP)], axis=0).astype(BF16)
        scores.append(_dot_nt(qg, K))
        slope = jnp.zeros((GW, 1), F32)
        sink = jnp.zeros((GW, 1), F32)
        for g in range(SW_GROUP):
            slope = jnp.where(gcol == g, _alibi_slope(kh * SW_GROUP + g), slope)
            sink = jnp.where(gcol == g, sink_ref[sub, kh * SW_GROUP + g], sink)
        sinks.append((slope, sink))
    for kh in range(SW_KV_HEADS):
        slope, sink = sinks[kh]
        s = jnp.where(valid, scores[kh] * (SW_HD ** -0.5) - slope * relf, -jnp.inf)
        m = jnp.maximum(jnp.max(s, axis=-1, keepdims=True), sink)
        pv = _dot(jnp.exp(s - m).astype(BF16), vext[kh])
        og = pv[:, 0:SW_HD] / (pv[:, SW_HD:SW_HD + 1] + jnp.exp(sink - m))
        for g in range(SW_GROUP):
            h = kh * SW_GROUP + g
            o_ref[:, h * SW_HD:(h + 1) * SW_HD] = og[g * W:(g + 1) * W]


def _swa_prompt(q, k, v, sinks, sub, *, batch, seq_len):
    n = q.shape[0]
    nb = seq_len // WINDOW
    cur = lambda c: pl.BlockSpec((WINDOW, c), lambda b, t: (b * nb + t, 0))
    prev = lambda c: pl.BlockSpec((WINDOW, c), lambda b, t: (b * nb + jnp.maximum(t - 1, 0), 0))
    kvc = SW_KV_HEADS * SW_HD
    return pl.pallas_call(
        functools.partial(_swa_prompt_kernel, sub=sub),
        grid=(batch, nb),
        in_specs=[pl.BlockSpec(memory_space=pltpu.SMEM), cur(SW_HEADS * SW_HD), cur(kvc), cur(kvc), prev(kvc), prev(kvc)],
        out_specs=cur(SW_HEADS * SW_HD),
        out_shape=jax.ShapeDtypeStruct((n, SW_HEADS * SW_HD), F32),
        compiler_params=_cparams(("parallel", "parallel")),
        name="swa_prompt",
    )(sinks, q, k, v, k, v)


SEQ_BLK = 8


def _state_call(kernel_fn, name, sub, stacked_states, prev_outs, other_ins, other_specs, row_out_shape, row_out_spec,
                smem_ins=()):
    nstate = len(stacked_states)

    def sspec(a):
        nd = a.ndim - 2
        return pl.BlockSpec((None, SEQ_BLK) + a.shape[2:], lambda i: (sub, i) + (0,) * nd)

    ns = stacked_states[0].shape[1]
    ins = list(smem_ins) + list(stacked_states) + list(other_ins)
    specs = ([pl.BlockSpec(memory_space=pltpu.SMEM)] * len(smem_ins) + [sspec(a) for a in stacked_states]
             + list(other_specs))
    aliases = {}
    if prev_outs is not None:
        for j, p in enumerate(prev_outs):
            aliases[len(ins)] = 1 + j
            ins.append(p)
            specs.append(pl.BlockSpec(memory_space=pl.ANY))

    def body(*refs):
        n_in = len(smem_ins) + nstate + len(other_ins)
        kernel_fn(*refs[:n_in], *refs[len(ins):])

    return pl.pallas_call(
        body,
        grid=(ns // SEQ_BLK,),
        in_specs=specs,
        out_specs=[row_out_spec] + [sspec(a) for a in stacked_states],
        out_shape=[row_out_shape] + [jax.ShapeDtypeStruct(a.shape, F32) for a in stacked_states],
        input_output_aliases=aliases,
        compiler_params=_cparams(("parallel",)),
        name=name,
    )(*ins)


def _cols_by_block(a):
    n, c = a.shape
    return jnp.swapaxes(a.reshape(n // SEQ_BLK, SEQ_BLK, c), 1, 2)


def _dn_decode_kernel(gb_ref, s_ref, qT_ref, kT_ref, v_ref, o_ref, sn_ref):
    i = pl.program_id(0)
    for j in range(SEQ_BLK):
        for hh in range(DN_HEADS):
            sl = slice(hh * DN_DK, (hh + 1) * DN_DK)
            S = s_ref[j, hh]
            kcol = kT_ref[0, sl, j:j + 1]
            qcol = qT_ref[0, sl, j:j + 1]
            vrow = v_ref[j:j + 1, sl]
            a = jnp.exp(jnp.full((1, DN_DV), gb_ref[i * SEQ_BLK + j, hh], F32))
            beta = gb_ref[i * SEQ_BLK + j, DN_HEADS + hh]
            Sa = S * a
            pred = jnp.sum(Sa * kcol, axis=0, keepdims=True)
            Sn = Sa + kcol * (beta * (vrow - pred))
            sn_ref[j, hh] = Sn
            o_ref[j:j + 1, sl] = jnp.sum(Sn * qcol, axis=0, keepdims=True)


def _dn_decode(S_all, prev, sub, q, k, v, gb):
    n = q.shape[0]
    hd = DN_HEADS * DN_DK
    colspec = pl.BlockSpec((1, hd, SEQ_BLK), lambda i: (i, 0, 0))
    rspec = pl.BlockSpec((SEQ_BLK, hd), lambda i: (i, 0))
    return _state_call(_dn_decode_kernel, "dn_decode", sub, [S_all], prev,
                       [_cols_by_block(q), _cols_by_block(k), v], [colspec, colspec, rspec],
                       jax.ShapeDtypeStruct((n, hd), F32), rspec, smem_ins=[gb])


def _gla_decode_kernel(s_ref, qT_ref, kT_ref, glT_ref, v_ref, o_ref, sn_ref):
    for j in range(SEQ_BLK):
        for hh in range(GLA_HEADS):
            ks = slice(hh * GLA_DK, (hh + 1) * GLA_DK)
            vs = slice(hh * GLA_DV, (hh + 1) * GLA_DV)
            S = s_ref[j, hh]
            kcol = kT_ref[0, ks, j:j + 1]
            qcol = qT_ref[0, ks, j:j + 1]
            acol = jnp.exp(glT_ref[0, ks, j:j + 1])
            vrow = v_ref[j:j + 1, vs]
            Sn = S * acol + kcol * vrow
            sn_ref[j, hh] = Sn
            o_ref[j:j + 1, vs] = jnp.sum(Sn * qcol, axis=0, keepdims=True)


def _gla_decode(S_all, prev, sub, q, k, v, gl):
    n = q.shape[0]
    hk = GLA_HEADS * GLA_DK
    hv = GLA_HEADS * GLA_DV
    colspec = pl.BlockSpec((1, hk, SEQ_BLK), lambda i: (i, 0, 0))
    rspec = pl.BlockSpec((SEQ_BLK, hv), lambda i: (i, 0))
    return _state_call(_gla_decode_kernel, "gla_decode", sub, [S_all], prev,
                       [_cols_by_block(q), _cols_by_block(k), _cols_by_block(gl), v],
                       [colspec, colspec, colspec, rspec],
                       jax.ShapeDtypeStruct((n, hv), F32), rspec)


def _swa_decode_kernel(sink_ref, kc_ref, vc_ref, q_ref, kn_ref, vn_ref, knT_ref, vnT_ref, o_ref, ko_ref, vo_ref, *, sub):
    W = WINDOW
    wi = lax.broadcasted_iota(jnp.int32, (SW_GROUP, W), 1)
    relf = (W - wi).astype(F32)
    valid = wi >= 1
    gi = lax.broadcasted_iota(jnp.int32, (SW_GROUP, 1), 0)
    last = lax.broadcasted_iota(jnp.int32, (SW_HD, W), 1) == W - 1

    def shifted(cache, new_col):
        return jnp.where(last, new_col, pltpu.roll(cache, W - 1, 1))

    heads = []
    for kh in range(SW_KV_HEADS):
        slope = jnp.zeros((SW_GROUP, 1), F32)
        sink = jnp.zeros((SW_GROUP, 1), F32)
        for g in range(SW_GROUP):
            slope = jnp.where(gi == g, _alibi_slope(kh * SW_GROUP + g), slope)
            sink = jnp.where(gi == g, sink_ref[sub, kh * SW_GROUP + g], sink)
        heads.append((slope, sink))
    ids = [(j, kh) for j in range(SEQ_BLK) for kh in range(SW_KV_HEADS)]
    U = {}
    for (j, kh) in ids:
        sl = slice(kh * SW_HD, (kh + 1) * SW_HD)
        qg = jnp.concatenate(
            [q_ref[j:j + 1, (kh * SW_GROUP + g) * SW_HD:(kh * SW_GROUP + g + 1) * SW_HD] for g in range(SW_GROUP)],
            axis=0)
        kt = kc_ref[j, kh]
        s = _dot(qg.astype(BF16), kt.astype(BF16))
        ko_ref[j, kh] = shifted(kt, knT_ref[0, sl, j:j + 1])
        U[j, kh] = (s, jnp.sum(qg * kn_ref[j:j + 1, sl], axis=-1, keepdims=True) * (SW_HD ** -0.5))
    for (j, kh) in ids:
        slope, sink = heads[kh]
        s, s_new = U[j, kh]
        s = jnp.where(valid, s * (SW_HD ** -0.5) - slope * relf, -jnp.inf)
        m = jnp.maximum(jnp.maximum(jnp.max(s, axis=-1, keepdims=True), s_new), sink)
        p = jnp.exp(s - m)
        p_new = jnp.exp(s_new - m)
        den = jnp.sum(p, axis=-1, keepdims=True) + p_new + jnp.exp(sink - m)
        U[j, kh] = ((p / den).astype(BF16), p_new / den)
    for (j, kh) in ids:
        sl = slice(kh * SW_HD, (kh + 1) * SW_HD)
        p16, w_new = U[j, kh]
        vt = vc_ref[j, kh]
        og = _dot_nt(p16, vt.astype(BF16)) + w_new * vn_ref[j:j + 1, sl]
        vo_ref[j, kh] = shifted(vt, vnT_ref[0, sl, j:j + 1])
        for g in range(SW_GROUP):
            h = kh * SW_GROUP + g
            o_ref[j:j + 1, h * SW_HD:(h + 1) * SW_HD] = og[g:g + 1]


def _swa_decode(kc_all, vc_all, prev, sub, q, kn, vn, sinks):
    n = q.shape[0]
    kvc = SW_KV_HEADS * SW_HD
    rq = pl.BlockSpec((SEQ_BLK, SW_HEADS * SW_HD), lambda i: (i, 0))
    rkv = pl.BlockSpec((SEQ_BLK, kvc), lambda i: (i, 0))
    ckv = pl.BlockSpec((1, kvc, SEQ_BLK), lambda i: (i, 0, 0))
    return _state_call(functools.partial(_swa_decode_kernel, sub=sub), "swa_decode", sub, [kc_all, vc_all], prev,
                       [q, kn, vn, _cols_by_block(kn), _cols_by_block(vn)], [rq, rkv, rkv, ckv, ckv],
                       jax.ShapeDtypeStruct((n, SW_HEADS * SW_HD), F32), rq, smem_ins=[sinks])


def _pad_last(a, width):
    return jnp.pad(a, [(0, 0)] * (a.ndim - 1) + [(0, width - a.shape[-1])])


def kernel(x_prompt, x_sample, state_dn, state_dn_conv, cache_swa_k, cache_swa_v, state_gla,
           norm_mix, norm_mlp, norm_final,
           even_w_in, dn_conv_w, dn_a_log, dn_dt_bias, dn_norm, sw_sinks, even_w_out,
           gla_w_in, gla_w_gate_up, gla_b_gate, gla_norm, gla_w_out,
           mlp_w_up, mlp_w_down):
    B, L, D = x_prompt.shape
    NS = x_sample.shape[0]
    depth = norm_mix.shape[0]
    n_even = even_w_in.shape[0]
    n_odd = gla_w_in.shape[0]
    kvc = SW_KV_HEADS * SW_HD
    xp = x_prompt.reshape(B * L, D)
    xs = x_sample.reshape(NS, D)
    tm_p = 512
    tc_dn = 256
    tc_gla = 256
    no_hist = jnp.zeros((SUBLANES, LANES), F32)

    nmix = norm_mix.reshape(depth, 1, D)
    nmlp = norm_mlp.reshape(depth, 1, D)
    nfin = norm_final.reshape(1, D)
    wu = mlp_w_up.astype(BF16)
    wd = mlp_w_down.astype(BF16)
    ab0 = EV_QKV + 512
    sw0 = ab0 + 2 * DN_HEADS
    w_ab = even_w_in[:, :, ab0:sw0]
    ev_w = jnp.concatenate(
        [even_w_in[:, :, :ab0], even_w_in[:, :, sw0:], _pad_last(w_ab, LANES),
         jnp.repeat(w_ab[:, :, 0:DN_HEADS], CHUNK, axis=2), jnp.repeat(w_ab[:, :, DN_HEADS:], CHUNK, axis=2)],
        axis=2).astype(BF16)
    alog = _pad_last(dn_a_log, LANES).reshape(n_even, 1, LANES)
    dtb = _pad_last(dn_dt_bias, LANES).reshape(n_even, 1, LANES)
    alogx = jnp.repeat(dn_a_log, CHUNK, axis=1).reshape(n_even, 1, CAT)
    dtbx = jnp.repeat(dn_dt_bias, CHUNK, axis=1).reshape(n_even, 1, CAT)
    dn_gn = dn_norm.reshape(n_even, 1, DN_DV)
    ev_wo = even_w_out.astype(BF16)
    sinks = sw_sinks.astype(F32)
    od_w = jnp.concatenate([gla_w_in[:, :, :OD_G], _pad_last(gla_w_in[:, :, OD_G:], LANES)], axis=2).astype(BF16)
    od_wg = jnp.pad(gla_w_gate_up, ((0, 0), (0, LANES - GLA_RANK), (0, 0))).astype(BF16)
    od_bg = gla_b_gate.reshape(n_odd, 1, -1)
    gla_gn = gla_norm.reshape(n_odd, 1, GLA_DV)
    od_wo = gla_w_out.astype(BF16)
    kc_all = jnp.transpose(cache_swa_k, (0, 1, 3, 4, 2))
    vc_all = jnp.transpose(cache_swa_v, (0, 1, 3, 4, 2))

    p_dn, p_conv, p_k, p_v, p_gla, s_conv = [], [], [], [], [], []
    s_dn = s_kv = s_gla = None
    for layer in range(depth):
        nf = nfin if layer == depth - 1 else None
        if layer % 2 == 0:
            e = layer // 2
            q, k, v, z, gb, gx, bx, qsw, ksw, vsw, tail = _even_in(
                xp, nmix, layer, ev_w, dn_conv_w, alog, dtb, alogx, dtbx, e, no_hist,
                tm=tm_p, seq_len=L, per_token_hist=False)
            o_dn, S = _dn_prompt(q, k, v, gb, gx, bx, batch=B, seq_len=L, tc=tc_dn)
            o_sw = _swa_prompt(qsw, ksw, vsw, sinks, e, batch=B, seq_len=L)
            xp = _post_mlp(xp, (o_dn, z, o_sw), dn_gn, ev_wo, nmlp, wu, wd, nf, layer, e, tm=tm_p, even=True)
            p_dn.append(S)
            tiles = L // tm_p
            p_conv.append(tail.reshape(B, tiles, SUBLANES, EV_QKV)[:, tiles - 1, SUBLANES - (DN_CONV - 1):])
            p_k.append(ksw.reshape(B, L, SW_KV_HEADS, SW_HD)[:, L - WINDOW:])
            p_v.append(vsw.reshape(B, L, SW_KV_HEADS, SW_HD)[:, L - WINDOW:])
            hist = jnp.swapaxes(state_dn_conv[e], 0, 1)
            q, k, v, z, gb, gx, bx, qsw, ksw, vsw, upre = _even_in(
                xs, nmix, layer, ev_w, dn_conv_w, alog, dtb, alogx, dtbx, e, hist,
                tm=NS, seq_len=1, per_token_hist=True)
            o_dn, s_dn_new = _dn_decode(state_dn, None if s_dn is None else [s_dn], e, q, k, v, gb)
            s_dn = s_dn_new
            o_sw, nk, nv = _swa_decode(kc_all, vc_all, s_kv, e, qsw, ksw, vsw, sinks)
            s_kv = [nk, nv]
            xs = _post_mlp(xs, (o_dn, z, o_sw), dn_gn, ev_wo, nmlp, wu, wd, nf, layer, e, tm=NS, even=True)
            s_conv.append(jnp.concatenate([state_dn_conv[e][:, 1:], upre[:, None, :]], axis=1))
        else:
            o_i = layer // 2
            q, k, v, r, gl = _odd_in(xp, nmix, layer, od_w, od_wg, od_bg, o_i, tm=tm_p)
            o, St = _gla_prompt(q, k, v, gl, batch=B, seq_len=L, tc=tc_gla)
            xp = _post_mlp(xp, (o, r), gla_gn, od_wo, nmlp, wu, wd, nf, layer, o_i, tm=tm_p, even=False)
            p_gla.append(jnp.swapaxes(St, 2, 3))
            q, k, v, r, gl = _odd_in(xs, nmix, layer, od_w, od_wg, od_bg, o_i, tm=NS)
            o, s_gla_new = _gla_decode(state_gla, None if s_gla is None else [s_gla], o_i, q, k, v, gl)
            s_gla = s_gla_new
            xs = _post_mlp(xs, (o, r), gla_gn, od_wo, nmlp, wu, wd, nf, layer, o_i, tm=NS, even=False)

    to_cache = lambda a: jnp.transpose(a, (0, 1, 4, 2, 3))
    return (xp.reshape(B, L, D), xs.reshape(NS, 1, D),
            jnp.stack(p_dn), jnp.stack(p_conv), jnp.stack(p_k), jnp.stack(p_v), jnp.stack(p_gla),
            s_dn, jnp.stack(s_conv), to_cache(s_kv[0]), to_cache(s_kv[1]), s_gla)
```

```python
import functools

import jax
import jax.numpy as jnp
from jax import lax
from jax.experimental import pallas as pl
from jax.experimental.pallas import tpu as pltpu

F32 = jnp.float32
BF16 = jnp.bfloat16
EPS = 1e-6

DN_HEADS = 4
DN_DK = 128
DN_DV = 128
DN_CONV = 4
SW_HEADS = 8
SW_KV_HEADS = 2
SW_GROUP = SW_HEADS // SW_KV_HEADS
SW_HD = 64
WINDOW = 128
GLA_HEADS = 4
GLA_DK = 128
GLA_DV = 256
GLA_RANK = 16
GLA_TAU = 16.0

LANES = 128
SUBLANES = 8
CHUNK = 64
LOG2E = 1.4426950408889634
VMEM_LIMIT = 56 * 1024 * 1024
CAT = DN_HEADS * CHUNK


def _dot(a, b):
    return jnp.dot(a, b, preferred_element_type=F32)


def _dot_nt(a, b):
    return lax.dot_general(a, b, (((1,), (1,)), ((), ())), preferred_element_type=F32)


def _dot_tn(a, b):
    return lax.dot_general(a, b, (((0,), (0,)), ((), ())), preferred_element_type=F32)


def _dot_f32(a, b):
    return jnp.dot(a, b, preferred_element_type=F32, precision=lax.Precision.HIGHEST)


def _rms(x, g):
    return x * lax.rsqrt(jnp.mean(x * x, axis=-1, keepdims=True) + EPS) * g


def _silu(x):
    return x * jax.nn.sigmoid(x)


def _softplus(x):
    return jnp.maximum(x, 0.0) + jnp.log1p(jnp.exp(-jnp.abs(x)))


def _log_sigmoid(x):
    return jnp.minimum(x, 0.0) - jnp.log1p(jnp.exp(-jnp.abs(x)))


def _cparams(sem):
    return pltpu.CompilerParams(dimension_semantics=sem, vmem_limit_bytes=VMEM_LIMIT)


def _const_spec(shape):
    nd = len(shape)
    return pl.BlockSpec(shape, lambda *_: (0,) * nd)


def _layer_spec(a, layer):
    nd = a.ndim - 1
    return pl.BlockSpec((None,) + a.shape[1:], lambda *_: (layer,) + (0,) * nd)


def _seg_cumsum(x, seg, axis):
    pos = lax.broadcasted_iota(jnp.int32, x.shape, axis) & (seg - 1)
    s = 1
    while s < seg:
        x = x + jnp.where(pos >= s, pltpu.roll(x, s, axis), 0.0)
        s *= 2
    return x


EV_QKV = 3 * DN_HEADS * DN_DK
EV_Z = EV_QKV
EV_QSW = EV_Z + 512
EV_KSW = EV_QSW + 512
EV_VSW = EV_KSW + 128
EV_AB = EV_VSW + 128
EV_AX = EV_AB + LANES
EV_BX = EV_AX + CAT
EV_COLS = EV_BX + CAT


def _even_in_kernel(x_ref, nw_ref, w_ref, cw_ref, alog_ref, dtb_ref, alogx_ref, dtbx_ref, hist_ref,
                    q_ref, k_ref, v_ref, z_ref, gb_ref, gx_ref, bx_ref, qsw_ref, ksw_ref, vsw_ref, upre_ref, kvt_ref,
                    carry_ref, *, tiles_per_seq, per_token_hist):
    i = pl.program_id(0)
    tm = x_ref.shape[0]
    h = _rms(x_ref[...], nw_ref[...]).astype(BF16)

    if not per_token_hist:
        @pl.when(i % tiles_per_seq == 0)
        def _():
            carry_ref[0:SUBLANES, :] = jnp.zeros((SUBLANES, EV_QKV), F32)

    def conv_section(u, c0):
        cw = cw_ref[:, c0:c0 + 512]
        if per_token_hist:
            upre_ref[:, c0:c0 + 512] = u
            out = u * cw[3:4]
            for t in range(DN_CONV - 1):
                out = out + hist_ref[t, :, c0:c0 + 512] * cw[t:t + 1]
            return out
        carry_ref[SUBLANES:SUBLANES + tm, c0:c0 + 512] = u
        out = u * cw[3:4]
        for s in range(1, DN_CONV):
            out = out + carry_ref[SUBLANES - s:SUBLANES - s + tm, c0:c0 + 512] * cw[3 - s:4 - s]
        carry_ref[0:SUBLANES, c0:c0 + 512] = u[tm - SUBLANES:tm]
        upre_ref[:, c0:c0 + 512] = u[tm - SUBLANES:tm]
        return out

    def l2n(y):
        outs = []
        for hh in range(DN_HEADS):
            yh = y[:, hh * DN_DK:(hh + 1) * DN_DK]
            outs.append(yh * lax.rsqrt(jnp.sum(yh * yh, axis=-1, keepdims=True) + EPS))
        return jnp.concatenate(outs, axis=1)

    q_ref[...] = l2n(_silu(conv_section(_dot(h, w_ref[:, 0:512]), 0))) * (DN_DK ** -0.5)
    k_ref[...] = l2n(_silu(conv_section(_dot(h, w_ref[:, 512:1024]), 512)))
    v_ref[...] = _silu(conv_section(_dot(h, w_ref[:, 1024:1536]), 1024))
    z_ref[...] = _dot(h, w_ref[:, EV_Z:EV_Z + 512])
    qsw_ref[...] = _dot(h, w_ref[:, EV_QSW:EV_QSW + 512])
    ksw = _dot(h, w_ref[:, EV_KSW:EV_KSW + 128])
    vsw = _dot(h, w_ref[:, EV_VSW:EV_VSW + 128])
    ksw_ref[...] = ksw
    vsw_ref[...] = vsw
    tw = kvt_ref.shape[0]
    kvt_ref[:, 0:128] = ksw[tm - tw:tm]
    kvt_ref[:, 128:256] = vsw[tm - tw:tm]
    ab = _dot(h, w_ref[:, EV_AB:EV_AB + LANES])
    g = -jnp.exp(alog_ref[...]) * _softplus(ab + dtb_ref[...])
    lane = lax.broadcasted_iota(jnp.int32, ab.shape, 1)
    gb = jnp.where(lane < DN_HEADS, g, jax.nn.sigmoid(ab))
    gb_ref[...] = gb[:, 0:2 * DN_HEADS]
    ax = _dot(h, w_ref[:, EV_AX:EV_AX + CAT])
    gx_ref[...] = -jnp.exp(alogx_ref[...]) * _softplus(ax + dtbx_ref[...])
    bx_ref[...] = jax.nn.sigmoid(_dot(h, w_ref[:, EV_BX:EV_BX + CAT]))


def _even_in(x, nw, layer, w, cw, alog, dtb, alogx, dtbx, sub, hist, *, tm, seq_len, per_token_hist):
    n, d = x.shape
    nt = n // tm
    tiles_per_seq = max(seq_len // tm, 1)
    row = lambda c: pl.BlockSpec((tm, c), lambda i: (i, 0))
    cols = (512, 512, 512, 512, 2 * DN_HEADS, CAT, CAT, 512, 128, 128)
    out_shapes = [jax.ShapeDtypeStruct((n, c), F32) for c in cols]
    out_specs = [row(c) for c in cols]
    if per_token_hist:
        hist_spec = pl.BlockSpec((DN_CONV - 1, tm, EV_QKV), lambda i: (0, i, 0))
        out_shapes.append(jax.ShapeDtypeStruct((n, EV_QKV), F32))
        out_specs.append(row(EV_QKV))
    else:
        hist_spec = _const_spec(hist.shape)
        out_shapes.append(jax.ShapeDtypeStruct((nt, SUBLANES, EV_QKV), F32))
        out_specs.append(pl.BlockSpec((None, SUBLANES, EV_QKV), lambda i: (i, 0, 0)))
    tw = min(WINDOW, tm)
    out_shapes.append(jax.ShapeDtypeStruct((nt, tw, 2 * SW_KV_HEADS * SW_HD), F32))
    out_specs.append(pl.BlockSpec((None, tw, 2 * SW_KV_HEADS * SW_HD), lambda i: (i, 0, 0)))
    return pl.pallas_call(
        functools.partial(_even_in_kernel, tiles_per_seq=tiles_per_seq, per_token_hist=per_token_hist),
        grid=(nt,),
        in_specs=([row(d), _layer_spec(nw, layer)] + [_layer_spec(a, sub) for a in (w, cw, alog, dtb, alogx, dtbx)]
                  + [hist_spec]),
        out_specs=out_specs,
        out_shape=out_shapes,
        scratch_shapes=[pltpu.VMEM((SUBLANES + (0 if per_token_hist else tm), EV_QKV), F32)],
        compiler_params=_cparams(("arbitrary",)),
        name="even_in",
    )(x, nw, w, cw, alog, dtb, alogx, dtbx, hist)


OD_K = 512
OD_V = 1024
OD_R = 2048
OD_G = 3072
OD_COLS = OD_G + LANES


def _odd_in_kernel(x_ref, nw_ref, w_ref, wg_ref, bg_ref, q_ref, k_ref, v_ref, r_ref, gl_ref):
    h = _rms(x_ref[...], nw_ref[...]).astype(BF16)
    q_ref[...] = _dot(h, w_ref[:, 0:512]) * (GLA_DK ** -0.5)
    k_ref[...] = _dot(h, w_ref[:, OD_K:OD_K + 512])
    for c in range(2):
        v_ref[:, c * 512:(c + 1) * 512] = _dot(h, w_ref[:, OD_V + c * 512:OD_V + (c + 1) * 512])
        r_ref[:, c * 512:(c + 1) * 512] = _dot(h, w_ref[:, OD_R + c * 512:OD_R + (c + 1) * 512])
    gd = _dot(h, w_ref[:, OD_G:OD_G + LANES])
    x = _dot(gd.astype(BF16), wg_ref[...]) + bg_ref[...]
    gl_ref[...] = _log_sigmoid(x) * (1.0 / GLA_TAU)


def _odd_in(x, nw, layer, w, wg, bg, sub, *, tm):
    n, d = x.shape
    row = lambda c: pl.BlockSpec((tm, c), lambda i: (i, 0))
    cols = (512, 512, 1024, 1024, 512)
    return pl.pallas_call(
        _odd_in_kernel,
        grid=(n // tm,),
        in_specs=[row(d), _layer_spec(nw, layer)] + [_layer_spec(a, sub) for a in (w, wg, bg)],
        out_specs=[row(c) for c in cols],
        out_shape=[jax.ShapeDtypeStruct((n, c), F32) for c in cols],
        compiler_params=_cparams(("parallel",)),
        name="odd_in",
    )(x, nw, w, wg, bg)


FF_CHUNK = 512


def _post_mlp_kernel(*refs, even, final):
    if even:
        x_ref, odn_ref, z_ref, osw_ref, gn_ref, wo_ref, nm_ref, wu_ref, wd_ref = refs[:9]
        rest = refs[9:]
    else:
        x_ref, o_ref, r_ref, gn_ref, wo_ref, nm_ref, wu_ref, wd_ref = refs[:8]
        rest = refs[8:]
    if final:
        nf_ref, y_ref = rest
    else:
        (y_ref,) = rest

    x = x_ref[...]
    gn = gn_ref[...]
    if even:
        parts = []
        for hh in range(DN_HEADS):
            sl = slice(hh * DN_DV, (hh + 1) * DN_DV)
            parts.append(_rms(odn_ref[:, sl], gn) * _silu(z_ref[:, sl]))
        a = jnp.concatenate(parts, axis=1).astype(BF16)
        nv = DN_HEADS * DN_DV
        mix = _dot(a, wo_ref[0:nv, :]) + _dot(osw_ref[...].astype(BF16), wo_ref[nv:, :])
    else:
        parts = []
        for hh in range(GLA_HEADS):
            sl = slice(hh * GLA_DV, (hh + 1) * GLA_DV)
            parts.append(_rms(o_ref[:, sl], gn) * _silu(r_ref[:, sl]))
        mix = _dot(jnp.concatenate(parts, axis=1).astype(BF16), wo_ref[...])
    x1 = x + mix
    h2 = _rms(x1, nm_ref[...]).astype(BF16)
    d_ff = wu_ref.shape[1]
    acc = x1
    for c in range(d_ff // FF_CHUNK):
        sl = slice(c * FF_CHUNK, (c + 1) * FF_CHUNK)
        up = jnp.maximum(_dot(h2, wu_ref[:, sl]), 0.0)
        acc = acc + _dot((up * up).astype(BF16), wd_ref[sl, :])
    if final:
        acc = _rms(acc, nf_ref[...])
    y_ref[...] = acc


def _post_mlp(x, mixer_outs, gn, wo, nm, wu, wd, nf, layer, sub, *, tm, even):
    n, d = x.shape
    row = lambda c: pl.BlockSpec((tm, c), lambda i: (i, 0))
    final = nf is not None
    ins = [x, *mixer_outs, gn, wo, nm, wu, wd]
    specs = ([row(d)] + [row(a.shape[1]) for a in mixer_outs]
             + [_layer_spec(gn, sub), _layer_spec(wo, sub)] + [_layer_spec(a, layer) for a in (nm, wu, wd)])
    if final:
        ins.append(nf)
        specs.append(_const_spec(nf.shape))
    return pl.pallas_call(
        functools.partial(_post_mlp_kernel, even=even, final=final),
        grid=(n // tm,),
        in_specs=specs,
        out_specs=row(d),
        out_shape=jax.ShapeDtypeStruct((n, d), F32),
        compiler_params=_cparams(("parallel",)),
        name="post_mlp_even" if even else "post_mlp_odd",
    )(*ins)


def _block_diag(p16, row_blk, col_blk):
    R, Cc = p16.shape
    t = jnp.concatenate([p16] * DN_HEADS, axis=0)
    ri = lax.broadcasted_iota(jnp.int32, t.shape, 0) // row_blk
    ci = lax.broadcasted_iota(jnp.int32, t.shape, 1) // col_blk
    return jnp.where(ri == ci, t, jnp.zeros_like(t))


def _widen(x):
    lane = lax.broadcasted_iota(jnp.int32, (x.shape[0], LANES), 1)
    outs = []
    for t in range(CAT // LANES):
        xt = x[:, t * LANES:(t + 1) * LANES]
        xr = pltpu.roll(xt, CHUNK, 1)
        outs.append(jnp.where(lane < CHUNK, xt, xr))
        outs.append(jnp.where(lane < CHUNK, xr, xt))
    return jnp.concatenate(outs, axis=1)


def _dn_prompt_kernel(q_ref, k_ref, v_ref, gx_ref, bx_ref, grow_ref, brow_ref, o_ref, s_ref, st_ref, *, chunks):
    t = pl.program_id(0)
    nb = q_ref.shape[0]

    @pl.when(t == 0)
    def _():
        st_ref[...] = jnp.zeros_like(st_ref)

    C = CHUNK
    ri = lax.broadcasted_iota(jnp.int32, (C, CAT), 0)
    cj = lax.broadcasted_iota(jnp.int32, (C, CAT), 1) & (C - 1)
    tri_incl = ri >= cj
    tri_strict = ri > cj
    eye = (ri == cj).astype(F32)
    n_sq = max((C - 1).bit_length() - 1, 0)

    Gcols = [_seg_cumsum(gx_ref[b], C, 0) for b in range(nb)]
    Grows = [_seg_cumsum(grow_ref[b], C, 1) for b in range(nb)]
    ids = [(b, c) for c in range(chunks) for b in range(nb)]
    U = {}
    for (b, c) in ids:
        rs = slice(c * C, (c + 1) * C)
        Gc = Gcols[b][rs]
        Gr = Grows[b][c:c + 1]
        decay = jnp.where(tri_incl, jnp.exp(jnp.where(tri_incl, Gc - Gr, 0.0)), 0.0)
        k = k_ref[b, rs, :]
        q = q_ref[b, rs, :]
        k16 = k.astype(BF16)
        kq = _dot_nt(jnp.concatenate([k16, q.astype(BF16)], axis=0), _block_diag(k16, C, DN_DK))
        M = jnp.where(tri_strict, kq[0:C] * bx_ref[b, rs, :] * decay, 0.0)
        U[b, c] = dict(Gc=Gc, Gr=Gr, k=k, q=q, k16=k16, attn=kq[C:] * decay, X=eye - M, P16=M.astype(BF16))
    for (b, c) in ids:
        u_ = U[b, c]
        u_["P16"] = _dot(u_["P16"], _block_diag(u_["P16"], C, C)).astype(BF16)
    for s in range(n_sq):
        for (b, c) in ids:
            u_ = U[b, c]
            bd = _block_diag(u_["P16"], C, C)
            if s + 1 < n_sq:
                y = _dot(jnp.concatenate([u_["X"].astype(BF16), u_["P16"]], axis=0), bd)
                u_["X"] = u_["X"] + y[0:C]
                u_["P16"] = y[C:].astype(BF16)
            else:
                u_["X"] = u_["X"] + _dot(u_["X"].astype(BF16), bd)
    for (b, c) in ids:
        u_ = U[b, c]
        rs = slice(c * C, (c + 1) * C)
        Gc, Gr, k, q, k16, X = u_["Gc"], u_["Gr"], u_["k"], u_["q"], u_["k16"], u_["X"]
        brow = brow_ref[b, c:c + 1, :]
        Tu = (X * brow).astype(BF16)
        Tw = (X * (brow * jnp.exp(Gr))).astype(BF16)
        g_last = Gc[C - 1:C, :]
        khat = (k * _widen(jnp.exp(g_last - Gc))).astype(BF16)
        qe = (q * _widen(jnp.exp(Gc))).astype(BF16)
        v16 = v_ref[b, rs, :].astype(BF16)
        heads = []
        for hh in range(DN_HEADS):
            sl = slice(hh * DN_DK, (hh + 1) * DN_DK)
            cs = slice(hh * C, (hh + 1) * C)
            u = _dot(Tu[:, cs], v16[:, sl])
            w = _dot(Tw[:, cs], k16[:, sl])
            heads.append((u, jnp.concatenate([w.astype(BF16), qe[:, sl]], axis=0),
                          u_["attn"][:, cs].astype(BF16), khat[:, sl], jnp.exp(g_last[:, hh * C:hh * C + 1])))
        U[b, c] = heads

    state = [[st_ref[b, hh] for hh in range(DN_HEADS)] for b in range(nb)]
    streams = [(b, hh) for b in range(nb) for hh in range(DN_HEADS)]
    for c in range(chunks):
        rs = slice(c * C, (c + 1) * C)
        R, vn16 = {}, {}
        for (b, hh) in streams:
            R[b, hh] = _dot(U[b, c][hh][1], state[b][hh].astype(BF16))
        for (b, hh) in streams:
            vn16[b, hh] = (U[b, c][hh][0] - R[b, hh][0:C]).astype(BF16)
            o_ref[b, rs, hh * DN_DK:(hh + 1) * DN_DK] = R[b, hh][C:] + _dot(U[b, c][hh][2], vn16[b, hh])
        for (b, hh) in streams:
            state[b][hh] = state[b][hh] * U[b, c][hh][4] + _dot_tn(U[b, c][hh][3], vn16[b, hh])
    for b in range(nb):
        for hh in range(DN_HEADS):
            st_ref[b, hh] = state[b][hh]

    @pl.when(t == pl.num_programs(0) - 1)
    def _():
        s_ref[...] = st_ref[...]


def _dn_prompt(q, k, v, gb, gx, bx, *, batch, seq_len, tc):
    n = q.shape[0]
    nt = seq_len // tc
    chunks = tc // CHUNK
    hd = DN_HEADS * DN_DK
    gb5 = jnp.swapaxes(gb.reshape(batch, nt, chunks, CHUNK, 2 * DN_HEADS), 3, 4)
    grow = gb5[:, :, :, 0:DN_HEADS].reshape(batch, nt, chunks, CAT)
    brow = gb5[:, :, :, DN_HEADS:].reshape(batch, nt, chunks, CAT)
    r3 = lambda a: a.reshape(batch, seq_len, a.shape[-1])
    tok = lambda c: pl.BlockSpec((batch, tc, c), lambda t: (0, t, 0))
    rowspec = pl.BlockSpec((batch, None, chunks, CAT), lambda t: (0, t, 0, 0))
    o, s = pl.pallas_call(
        functools.partial(_dn_prompt_kernel, chunks=chunks),
        grid=(nt,),
        in_specs=[tok(hd), tok(hd), tok(hd), tok(CAT), tok(CAT), rowspec, rowspec],
        out_specs=[tok(hd), _const_spec((batch, DN_HEADS, DN_DK, DN_DV))],
        out_shape=[jax.ShapeDtypeStruct((batch, seq_len, hd), F32),
                   jax.ShapeDtypeStruct((batch, DN_HEADS, DN_DK, DN_DV), F32)],
        scratch_shapes=[pltpu.VMEM((batch, DN_HEADS, DN_DK, DN_DV), F32)],
        compiler_params=_cparams(("arbitrary",)),
        name="dn_prompt",
    )(r3(q), r3(k), r3(v), r3(gx), r3(bx), grow, brow)
    return o.reshape(n, hd), s


def _row_in_tile(x, r):
    R, Ln = x.shape
    x3 = x.reshape(R // SUBLANES, SUBLANES, Ln)
    return jnp.broadcast_to(x3[:, r:r + 1, :], x3.shape).reshape(R, Ln)


def _gla_intra(q, k, G2, sub_i, tile_lower):
    C = q.shape[0]
    nt = C // SUBLANES
    zrow = lambda n: jnp.zeros((n, GLA_DK), BF16)
    qs = (q * jnp.exp2(G2 - _row_in_tile(G2, 0))).astype(BF16)
    q16 = q.astype(BF16)
    q1, k1 = [], []
    for I in range(1, nt):
        lo = I * SUBLANES
        q1.append(jnp.concatenate([zrow(lo), qs[lo:lo + SUBLANES]] + ([zrow(C - lo - SUBLANES)] if C - lo - SUBLANES else []),
                                  axis=0))
        k1.append(jnp.concatenate([(k[0:lo] * jnp.exp2(G2[lo:lo + 1] - G2[0:lo])).astype(BF16), zrow(C - lo)], axis=0))
    a1 = _dot_nt(jnp.concatenate(q1, axis=1), jnp.concatenate(k1, axis=1))
    q2, k2 = [], []
    for I in range(SUBLANES):
        q2.append(jnp.where(sub_i == I, q16, jnp.zeros_like(q16)))
        k2.append((k * jnp.exp2(jnp.minimum(_row_in_tile(G2, I) - G2, 0.0))).astype(BF16))
    a2 = _dot_nt(jnp.concatenate(q2, axis=1), jnp.concatenate(k2, axis=1))
    return a1 + jnp.where(tile_lower, a2, 0.0)


def _gla_prompt_kernel(q_ref, k_ref, v_ref, gl_ref, o_ref, s_ref, st_ref, *, chunks):
    t = pl.program_id(0)
    nb = q_ref.shape[0]

    @pl.when(t == 0)
    def _():
        st_ref[...] = jnp.zeros_like(st_ref)

    C = CHUNK
    sub_i = lax.broadcasted_iota(jnp.int32, (C, GLA_DK), 0) & (SUBLANES - 1)
    ri = lax.broadcasted_iota(jnp.int32, (C, C), 0)
    ci = lax.broadcasted_iota(jnp.int32, (C, C), 1)
    tile_lower = ((ri // SUBLANES) == (ci // SUBLANES)) & (ci <= ri)

    Gs = [_seg_cumsum(gl_ref[b] * LOG2E, C, 0) for b in range(nb)]
    ids = [(b, c, hh) for c in range(chunks) for b in range(nb) for hh in range(GLA_HEADS)]
    U = {}
    for (b, c, hh) in ids:
        rs = slice(c * C, (c + 1) * C)
        ks = slice(hh * GLA_DK, (hh + 1) * GLA_DK)
        q = q_ref[b, rs, ks]
        k = k_ref[b, rs, ks]
        G = Gs[b][rs, ks]
        A = _gla_intra(q, k, G, sub_i, tile_lower)
        gl = G[C - 1:C, :]
        U[b, c, hh] = dict(A=A.astype(BF16), qe=(q * jnp.exp2(G)).astype(BF16),
                           khat=(k * jnp.exp2(gl - G)).astype(BF16), e_last=jnp.exp2(gl))
    for (b, c, hh) in ids:
        u_ = U[b, c, hh]
        v16 = v_ref[b, c * C:(c + 1) * C, hh * GLA_DV:(hh + 1) * GLA_DV].astype(BF16)
        u_["o"] = _dot(u_["A"], v16)
        u_["B"] = _dot_tn(v16, u_["khat"])
    for b in range(nb):
        for hh in range(GLA_HEADS):
            St = st_ref[b, hh]
            for c in range(chunks):
                U[b, c, hh]["St"] = St.astype(BF16)
                St = St * U[b, c, hh]["e_last"] + U[b, c, hh]["B"]
            st_ref[b, hh] = St
    for (b, c, hh) in ids:
        u_ = U[b, c, hh]
        o_ref[b, c * C:(c + 1) * C, hh * GLA_DV:(hh + 1) * GLA_DV] = u_["o"] + _dot_nt(u_["qe"], u_["St"])

    @pl.when(t == pl.num_programs(0) - 1)
    def _():
        s_ref[...] = st_ref[...]


def _gla_prompt(q, k, v, gl, *, batch, seq_len, tc):
    n = q.shape[0]
    nt = seq_len // tc
    chunks = tc // CHUNK
    hk = GLA_HEADS * GLA_DK
    hv = GLA_HEADS * GLA_DV
    r3 = lambda a: a.reshape(batch, seq_len, a.shape[-1])
    tok = lambda c: pl.BlockSpec((batch, tc, c), lambda t: (0, t, 0))
    o, s = pl.pallas_call(
        functools.partial(_gla_prompt_kernel, chunks=chunks),
        grid=(nt,),
        in_specs=[tok(hk), tok(hk), tok(hv), tok(hk)],
        out_specs=[tok(hv), _const_spec((batch, GLA_HEADS, GLA_DV, GLA_DK))],
        out_shape=[jax.ShapeDtypeStruct((batch, seq_len, hv), F32),
                   jax.ShapeDtypeStruct((batch, GLA_HEADS, GLA_DV, GLA_DK), F32)],
        scratch_shapes=[pltpu.VMEM((batch, GLA_HEADS, GLA_DV, GLA_DK), F32)],
        compiler_params=_cparams(("arbitrary",)),
        name="gla_prompt",
    )(r3(q), r3(k), r3(v), r3(gl))
    return o.reshape(n, hv), s


def _alibi_slope(h):
    return 2.0 ** (-8.0 * (h + 1) / SW_HEADS)


def _swa_prompt_kernel(sink_ref, q_ref, kc_ref, vc_ref, kp_ref, vp_ref, o_ref, *, sub):
    n = pl.program_id(1)
    W = WINDOW
    GW = SW_GROUP * W
    ri = lax.broadcasted_iota(jnp.int32, (GW, 2 * W), 0)
    ci = lax.broadcasted_iota(jnp.int32, (GW, 2 * W), 1)
    rel = (ri & (W - 1)) + W - ci
    valid = (rel >= 0) & (rel < W) & ((ci >= W) | (n > 0))
    relf = rel.astype(F32)
    gcol = lax.broadcasted_iota(jnp.int32, (GW, 1), 0) // W
    ones = jnp.ones((2 * W, SW_HD), BF16)
    scores, vext, sinks = [], [], []
    for kh in range(SW_KV_HEADS):
        sl = slice(kh * SW_HD, (kh + 1) * SW_HD)
        K = jnp.concatenate([kp_ref[:, sl], kc_ref[:, sl]], axis=0).astype(BF16)
        V = jnp.concatenate([vp_ref[:, sl], vc_ref[:, sl]], axis=0).astype(BF16)
        vext.append(jnp.concatenate([V, ones], axis=1))
        qg = jnp.concatenate([q_ref[:, (kh * SW_GROUP + g) * SW_HD:(kh * SW_GROUP + g + 1) * SW_HD]
                              for g in range(SW_GROUP)], axis=0).astype(BF16)
        scores.append(_dot_nt(qg, K))
        slope = jnp.zeros((GW, 1), F32)
        sink = jnp.zeros((GW, 1), F32)
        for g in range(SW_GROUP):
            slope = jnp.where(gcol == g, _alibi_slope(kh * SW_GROUP + g), slope)
            sink = jnp.where(gcol == g, sink_ref[sub, kh * SW_GROUP + g], sink)
        sinks.append((slope, sink))
    for kh in range(SW_KV_HEADS):
        slope, sink = sinks[kh]
        s = jnp.where(valid, scores[kh] * (SW_HD ** -0.5) - slope * relf, -jnp.inf)
        m = jnp.maximum(jnp.max(s, axis=-1, keepdims=True), sink)
        pv = _dot(jnp.exp(s - m).astype(BF16), vext[kh])
        og = pv[:, 0:SW_HD] / (pv[:, SW_HD:SW_HD + 1] + jnp.exp(sink - m))
        for g in range(SW_GROUP):
            h = kh * SW_GROUP + g
            o_ref[:, h * SW_HD:(h + 1) * SW_HD] = og[g * W:(g + 1) * W]


def _swa_prompt(q, k, v, sinks, sub, *, batch, seq_len):
    n = q.shape[0]
    nb = seq_len // WINDOW
    cur = lambda c: pl.BlockSpec((WINDOW, c), lambda b, t: (b * nb + t, 0))
    prev = lambda c: pl.BlockSpec((WINDOW, c), lambda b, t: (b * nb + jnp.maximum(t - 1, 0), 0))
    kvc = SW_KV_HEADS * SW_HD
    return pl.pallas_call(
        functools.partial(_swa_prompt_kernel, sub=sub),
        grid=(batch, nb),
        in_specs=[pl.BlockSpec(memory_space=pltpu.SMEM), cur(SW_HEADS * SW_HD), cur(kvc), cur(kvc), prev(kvc), prev(kvc)],
        out_specs=cur(SW_HEADS * SW_HD),
        out_shape=jax.ShapeDtypeStruct((n, SW_HEADS * SW_HD), F32),
        compiler_params=_cparams(("parallel", "parallel")),
        name="swa_prompt",
    )(sinks, q, k, v, k, v)


SEQ_BLK = 8


def _state_call(kernel_fn, name, sub, stacked_states, prev_outs, other_ins, other_specs, row_out_shape, row_out_spec,
                smem_ins=()):
    nstate = len(stacked_states)

    def sspec(a):
        nd = a.ndim - 2
        return pl.BlockSpec((None, SEQ_BLK) + a.shape[2:], lambda i: (sub, i) + (0,) * nd)

    ns = stacked_states[0].shape[1]
    ins = list(smem_ins) + list(stacked_states) + list(other_ins)
    specs = ([pl.BlockSpec(memory_space=pltpu.SMEM)] * len(smem_ins) + [sspec(a) for a in stacked_states]
             + list(other_specs))
    aliases = {}
    if prev_outs is not None:
        for j, p in enumerate(prev_outs):
            aliases[len(ins)] = 1 + j
            ins.append(p)
            specs.append(pl.BlockSpec(memory_space=pl.ANY))

    def body(*refs):
        n_in = len(smem_ins) + nstate + len(other_ins)
        kernel_fn(*refs[:n_in], *refs[len(ins):])

    return pl.pallas_call(
        body,
        grid=(ns // SEQ_BLK,),
        in_specs=specs,
        out_specs=[row_out_spec] + [sspec(a) for a in stacked_states],
        out_shape=[row_out_shape] + [jax.ShapeDtypeStruct(a.shape, F32) for a in stacked_states],
        input_output_aliases=aliases,
        compiler_params=_cparams(("parallel",)),
        name=name,
    )(*ins)


def _cols_by_block(a):
    n, c = a.shape
    return jnp.swapaxes(a.reshape(n // SEQ_BLK, SEQ_BLK, c), 1, 2)


def _dn_decode_kernel(gb_ref, s_ref, qT_ref, kT_ref, v_ref, o_ref, sn_ref):
    i = pl.program_id(0)
    for j in range(SEQ_BLK):
        for hh in range(DN_HEADS):
            sl = slice(hh * DN_DK, (hh + 1) * DN_DK)
            S = s_ref[j, hh]
            kcol = kT_ref[0, sl, j:j + 1]
            qcol = qT_ref[0, sl, j:j + 1]
            vrow = v_ref[j:j + 1, sl]
            a = jnp.exp(jnp.full((1, DN_DV), gb_ref[i * SEQ_BLK + j, hh], F32))
            beta = gb_ref[i * SEQ_BLK + j, DN_HEADS + hh]
            Sa = S * a
            pred = jnp.sum(Sa * kcol, axis=0, keepdims=True)
            Sn = Sa + kcol * (beta * (vrow - pred))
            sn_ref[j, hh] = Sn
            o_ref[j:j + 1, sl] = jnp.sum(Sn * qcol, axis=0, keepdims=True)


def _dn_decode(S_all, prev, sub, q, k, v, gb):
    n = q.shape[0]
    hd = DN_HEADS * DN_DK
    colspec = pl.BlockSpec((1, hd, SEQ_BLK), lambda i: (i, 0, 0))
    rspec = pl.BlockSpec((SEQ_BLK, hd), lambda i: (i, 0))
    return _state_call(_dn_decode_kernel, "dn_decode", sub, [S_all], prev,
                       [_cols_by_block(q), _cols_by_block(k), v], [colspec, colspec, rspec],
                       jax.ShapeDtypeStruct((n, hd), F32), rspec, smem_ins=[gb])


def _gla_decode_kernel(s_ref, qT_ref, kT_ref, glT_ref, v_ref, o_ref, sn_ref):
    for j in range(SEQ_BLK):
        for hh in range(GLA_HEADS):
            ks = slice(hh * GLA_DK, (hh + 1) * GLA_DK)
            vs = slice(hh * GLA_DV, (hh + 1) * GLA_DV)
            S = s_ref[j, hh]
            kcol = kT_ref[0, ks, j:j + 1]
            qcol = qT_ref[0, ks, j:j + 1]
            acol = jnp.exp(glT_ref[0, ks, j:j + 1])
            vrow = v_ref[j:j + 1, vs]
            Sn = S * acol + kcol * vrow
            sn_ref[j, hh] = Sn
            o_ref[j:j + 1, vs] = jnp.sum(Sn * qcol, axis=0, keepdims=True)


def _gla_decode(S_all, prev, sub, q, k, v, gl):
    n = q.shape[0]
    hk = GLA_HEADS * GLA_DK
    hv = GLA_HEADS * GLA_DV
    colspec = pl.BlockSpec((1, hk, SEQ_BLK), lambda i: (i, 0, 0))
    rspec = pl.BlockSpec((SEQ_BLK, hv), lambda i: (i, 0))
    return _state_call(_gla_decode_kernel, "gla_decode", sub, [S_all], prev,
                       [_cols_by_block(q), _cols_by_block(k), _cols_by_block(gl), v],
                       [colspec, colspec, colspec, rspec],
                       jax.ShapeDtypeStruct((n, hv), F32), rspec)


def _swa_decode_kernel(sink_ref, kc_ref, vc_ref, q_ref, kn_ref, vn_ref, knT_ref, vnT_ref, o_ref, ko_ref, vo_ref, *, sub):
    W = WINDOW
    wi = lax.broadcasted_iota(jnp.int32, (SW_GROUP, W), 1)
    relf = (W - wi).astype(F32)
    valid = wi >= 1
    gi = lax.broadcasted_iota(jnp.int32, (SW_GROUP, 1), 0)
    last = lax.broadcasted_iota(jnp.int32, (SW_HD, W), 1) == W - 1

    def shifted(cache, new_col):
        return jnp.where(last, new_col, pltpu.roll(cache, W - 1, 1))

    heads = []
    for kh in range(SW_KV_HEADS):
        slope = jnp.zeros((SW_GROUP, 1), F32)
        sink = jnp.zeros((SW_GROUP, 1), F32)
        for g in range(SW_GROUP):
            slope = jnp.where(gi == g, _alibi_slope(kh * SW_GROUP + g), slope)
            sink = jnp.where(gi == g, sink_ref[sub, kh * SW_GROUP + g], sink)
        heads.append((slope, sink))
    ids = [(j, kh) for j in range(SEQ_BLK) for kh in range(SW_KV_HEADS)]
    U = {}
    for (j, kh) in ids:
        sl = slice(kh * SW_HD, (kh + 1) * SW_HD)
        qg = jnp.concatenate(
            [q_ref[j:j + 1, (kh * SW_GROUP + g) * SW_HD:(kh * SW_GROUP + g + 1) * SW_HD] for g in range(SW_GROUP)],
            axis=0)
        kt = kc_ref[j, kh]
        s = _dot(qg.astype(BF16), kt.astype(BF16))
        ko_ref[j, kh] = shifted(kt, knT_ref[0, sl, j:j + 1])
        U[j, kh] = (s, jnp.sum(qg * kn_ref[j:j + 1, sl], axis=-1, keepdims=True) * (SW_HD ** -0.5))
    for (j, kh) in ids:
        slope, sink = heads[kh]
        s, s_new = U[j, kh]
        s = jnp.where(valid, s * (SW_HD ** -0.5) - slope * relf, -jnp.inf)
        m = jnp.maximum(jnp.maximum(jnp.max(s, axis=-1, keepdims=True), s_new), sink)
        p = jnp.exp(s - m)
        p_new = jnp.exp(s_new - m)
        den = jnp.sum(p, axis=-1, keepdims=True) + p_new + jnp.exp(sink - m)
        U[j, kh] = ((p / den).astype(BF16), p_new / den)
    for (j, kh) in ids:
        sl = slice(kh * SW_HD, (kh + 1) * SW_HD)
        p16, w_new = U[j, kh]
        vt = vc_ref[j, kh]
        og = _dot_nt(p16, vt.astype(BF16)) + w_new * vn_ref[j:j + 1, sl]
        vo_ref[j, kh] = shifted(vt, vnT_ref[0, sl, j:j + 1])
        for g in range(SW_GROUP):
            h = kh * SW_GROUP + g
            o_ref[j:j + 1, h * SW_HD:(h + 1) * SW_HD] = og[g:g + 1]


def _swa_decode(kc_all, vc_all, prev, sub, q, kn, vn, sinks):
    n = q.shape[0]
    kvc = SW_KV_HEADS * SW_HD
    rq = pl.BlockSpec((SEQ_BLK, SW_HEADS * SW_HD), lambda i: (i, 0))
    rkv = pl.BlockSpec((SEQ_BLK, kvc), lambda i: (i, 0))
    ckv = pl.BlockSpec((1, kvc, SEQ_BLK), lambda i: (i, 0, 0))
    return _state_call(functools.partial(_swa_decode_kernel, sub=sub), "swa_decode", sub, [kc_all, vc_all], prev,
                       [q, kn, vn, _cols_by_block(kn), _cols_by_block(vn)], [rq, rkv, rkv, ckv, ckv],
                       jax.ShapeDtypeStruct((n, SW_HEADS * SW_HD), F32), rq, smem_ins=[sinks])


def _pad_last(a, width):
    return jnp.pad(a, [(0, 0)] * (a.ndim - 1) + [(0, width - a.shape[-1])])


def kernel(x_prompt, x_sample, state_dn, state_dn_conv, cache_swa_k, cache_swa_v, state_gla,
           norm_mix, norm_mlp, norm_final,
           even_w_in, dn_conv_w, dn_a_log, dn_dt_bias, dn_norm, sw_sinks, even_w_out,
           gla_w_in, gla_w_gate_up, gla_b_gate, gla_norm, gla_w_out,
           mlp_w_up, mlp_w_down):
    B, L, D = x_prompt.shape
    NS = x_sample.shape[0]
    depth = norm_mix.shape[0]
    n_even = even_w_in.shape[0]
    n_odd = gla_w_in.shape[0]
    kvc = SW_KV_HEADS * SW_HD
    xp = x_prompt.reshape(B * L, D)
    xs = x_sample.reshape(NS, D)
    tm_p = 512
    tc_dn = 256
    tc_gla = 256
    no_hist = jnp.zeros((SUBLANES, LANES), F32)

    nmix = norm_mix.reshape(depth, 1, D)
    nmlp = norm_mlp.reshape(depth, 1, D)
    nfin = norm_final.reshape(1, D)
    wu = mlp_w_up.astype(BF16)
    wd = mlp_w_down.astype(BF16)
    ab0 = EV_QKV + 512
    sw0 = ab0 + 2 * DN_HEADS
    w_ab = even_w_in[:, :, ab0:sw0]
    ev_w = jnp.concatenate(
        [even_w_in[:, :, :ab0], even_w_in[:, :, sw0:], _pad_last(w_ab, LANES),
         jnp.repeat(w_ab[:, :, 0:DN_HEADS], CHUNK, axis=2), jnp.repeat(w_ab[:, :, DN_HEADS:], CHUNK, axis=2)],
        axis=2).astype(BF16)
    alog = _pad_last(dn_a_log, LANES).reshape(n_even, 1, LANES)
    dtb = _pad_last(dn_dt_bias, LANES).reshape(n_even, 1, LANES)
    alogx = jnp.repeat(dn_a_log, CHUNK, axis=1).reshape(n_even, 1, CAT)
    dtbx = jnp.repeat(dn_dt_bias, CHUNK, axis=1).reshape(n_even, 1, CAT)
    dn_gn = dn_norm.reshape(n_even, 1, DN_DV)
    ev_wo = even_w_out.astype(BF16)
    sinks = sw_sinks.astype(F32)
    od_w = jnp.concatenate([gla_w_in[:, :, :OD_G], _pad_last(gla_w_in[:, :, OD_G:], LANES)], axis=2).astype(BF16)
    od_wg = jnp.pad(gla_w_gate_up, ((0, 0), (0, LANES - GLA_RANK), (0, 0))).astype(BF16)
    od_bg = gla_b_gate.reshape(n_odd, 1, -1)
    gla_gn = gla_norm.reshape(n_odd, 1, GLA_DV)
    od_wo = gla_w_out.astype(BF16)
    kc_all = jnp.transpose(cache_swa_k, (0, 1, 3, 4, 2))
    vc_all = jnp.transpose(cache_swa_v, (0, 1, 3, 4, 2))

    p_dn, p_conv, p_k, p_v, p_gla, s_conv = [], [], [], [], [], []
    s_dn = s_kv = s_gla = None
    for layer in range(depth):
        nf = nfin if layer == depth - 1 else None
        if layer % 2 == 0:
            e = layer // 2
            q, k, v, z, gb, gx, bx, qsw, ksw, vsw, tail, kvt = _even_in(
                xp, nmix, layer, ev_w, dn_conv_w, alog, dtb, alogx, dtbx, e, no_hist,
                tm=tm_p, seq_len=L, per_token_hist=False)
            o_dn, S = _dn_prompt(q, k, v, gb, gx, bx, batch=B, seq_len=L, tc=tc_dn)
            o_sw = _swa_prompt(qsw, ksw, vsw, sinks, e, batch=B, seq_len=L)
            xp = _post_mlp(xp, (o_dn, z, o_sw), dn_gn, ev_wo, nmlp, wu, wd, nf, layer, e, tm=tm_p, even=True)
            p_dn.append(S)
            tiles = L // tm_p
            p_conv.append(tail.reshape(B, tiles, SUBLANES, EV_QKV)[:, tiles - 1, SUBLANES - (DN_CONV - 1):])
            kv_end = kvt.reshape(B, tiles, WINDOW, 2 * kvc)[:, tiles - 1]
            p_k.append(kv_end[:, :, 0:kvc].reshape(B, WINDOW, SW_KV_HEADS, SW_HD))
            p_v.append(kv_end[:, :, kvc:].reshape(B, WINDOW, SW_KV_HEADS, SW_HD))
            hist = jnp.swapaxes(state_dn_conv[e], 0, 1)
            q, k, v, z, gb, gx, bx, qsw, ksw, vsw, upre, _ = _even_in(
                xs, nmix, layer, ev_w, dn_conv_w, alog, dtb, alogx, dtbx, e, hist,
                tm=NS, seq_len=1, per_token_hist=True)
            o_dn, s_dn_new = _dn_decode(state_dn, None if s_dn is None else [s_dn], e, q, k, v, gb)
            s_dn = s_dn_new
            o_sw, nk, nv = _swa_decode(kc_all, vc_all, s_kv, e, qsw, ksw, vsw, sinks)
            s_kv = [nk, nv]
            xs = _post_mlp(xs, (o_dn, z, o_sw), dn_gn, ev_wo, nmlp, wu, wd, nf, layer, e, tm=NS, even=True)
            s_conv.append(jnp.concatenate([state_dn_conv[e][:, 1:], upre[:, None, :]], axis=1))
        else:
            o_i = layer // 2
            q, k, v, r, gl = _odd_in(xp, nmix, layer, od_w, od_wg, od_bg, o_i, tm=tm_p)
            o, St = _gla_prompt(q, k, v, gl, batch=B, seq_len=L, tc=tc_gla)
            xp = _post_mlp(xp, (o, r), gla_gn, od_wo, nmlp, wu, wd, nf, layer, o_i, tm=tm_p, even=False)
            p_gla.append(jnp.swapaxes(St, 2, 3))
            q, k, v, r, gl = _odd_in(xs, nmix, layer, od_w, od_wg, od_bg, o_i, tm=NS)
            o, s_gla_new = _gla_decode(state_gla, None if s_gla is None else [s_gla], o_i, q, k, v, gl)
            s_gla = s_gla_new
            xs = _post_mlp(xs, (o, r), gla_gn, od_wo, nmlp, wu, wd, nf, layer, o_i, tm=NS, even=False)

    to_cache = lambda a: jnp.transpose(a, (0, 1, 4, 2, 3))
    return (xp.reshape(B, L, D), xs.reshape(NS, 1, D),
            jnp.stack(p_dn), jnp.stack(p_conv), jnp.stack(p_k), jnp.stack(p_v), jnp.stack(p_gla),
            s_dn, jnp.stack(s_conv), to_cache(s_kv[0]), to_cache(s_kv[1]), s_gla)
```

```python
import functools

import jax
import jax.numpy as jnp
from jax import lax
from jax.experimental import pallas as pl
from jax.experimental.pallas import tpu as pltpu

F32 = jnp.float32
BF16 = jnp.bfloat16
EPS = 1e-6

DN_HEADS = 4
DN_DK = 128
DN_DV = 128
DN_CONV = 4
SW_HEADS = 8
SW_KV_HEADS = 2
SW_GROUP = SW_HEADS // SW_KV_HEADS
SW_HD = 64
WINDOW = 128
GLA_HEADS = 4
GLA_DK = 128
GLA_DV = 256
GLA_RANK = 16
GLA_TAU = 16.0

LANES = 128
SUBLANES = 8
CHUNK = 64
LOG2E = 1.4426950408889634
VMEM_LIMIT = 56 * 1024 * 1024
CAT = DN_HEADS * CHUNK


def _dot(a, b):
    return jnp.dot(a, b, preferred_element_type=F32)


def _dot_nt(a, b):
    return lax.dot_general(a, b, (((1,), (1,)), ((), ())), preferred_element_type=F32)


def _dot_tn(a, b):
    return lax.dot_general(a, b, (((0,), (0,)), ((), ())), preferred_element_type=F32)


def _dot_f32(a, b):
    return jnp.dot(a, b, preferred_element_type=F32, precision=lax.Precision.HIGHEST)


def _rms(x, g):
    return x * lax.rsqrt(jnp.mean(x * x, axis=-1, keepdims=True) + EPS) * g


def _silu(x):
    return x * jax.nn.sigmoid(x)


def _softplus(x):
    return jnp.maximum(x, 0.0) + jnp.log1p(jnp.exp(-jnp.abs(x)))


def _log_sigmoid(x):
    return jnp.minimum(x, 0.0) - jnp.log1p(jnp.exp(-jnp.abs(x)))


def _cparams(sem):
    return pltpu.CompilerParams(dimension_semantics=sem, vmem_limit_bytes=VMEM_LIMIT)


def _const_spec(shape):
    nd = len(shape)
    return pl.BlockSpec(shape, lambda *_: (0,) * nd)


def _layer_spec(a, layer):
    nd = a.ndim - 1
    return pl.BlockSpec((None,) + a.shape[1:], lambda *_: (layer,) + (0,) * nd)


def _seg_cumsum(x, seg, axis):
    pos = lax.broadcasted_iota(jnp.int32, x.shape, axis) & (seg - 1)
    s = 1
    while s < seg:
        x = x + jnp.where(pos >= s, pltpu.roll(x, s, axis), 0.0)
        s *= 2
    return x


EV_QKV = 3 * DN_HEADS * DN_DK
EV_Z = EV_QKV
EV_QSW = EV_Z + 512
EV_KSW = EV_QSW + 512
EV_VSW = EV_KSW + 128
EV_AB = EV_VSW + 128
EV_AX = EV_AB + LANES
EV_BX = EV_AX + CAT
EV_COLS = EV_BX + CAT


def _even_in_kernel(x_ref, nw_ref, w_ref, cw_ref, alog_ref, dtb_ref, alogx_ref, dtbx_ref, hist_ref,
                    q_ref, k_ref, v_ref, z_ref, gb_ref, gx_ref, bx_ref, qsw_ref, ksw_ref, vsw_ref, upre_ref, kvt_ref,
                    carry_ref, *, tiles_per_seq, per_token_hist):
    i = pl.program_id(0)
    tm = x_ref.shape[0]
    h = _rms(x_ref[...], nw_ref[...]).astype(BF16)

    if not per_token_hist:
        @pl.when(i % tiles_per_seq == 0)
        def _():
            carry_ref[0:SUBLANES, :] = jnp.zeros((SUBLANES, EV_QKV), F32)

    def conv_section(u, c0):
        cw = cw_ref[:, c0:c0 + 512]
        if per_token_hist:
            upre_ref[:, c0:c0 + 512] = u
            out = u * cw[3:4]
            for t in range(DN_CONV - 1):
                out = out + hist_ref[t, :, c0:c0 + 512] * cw[t:t + 1]
            return out
        carry_ref[SUBLANES:SUBLANES + tm, c0:c0 + 512] = u
        out = u * cw[3:4]
        for s in range(1, DN_CONV):
            out = out + carry_ref[SUBLANES - s:SUBLANES - s + tm, c0:c0 + 512] * cw[3 - s:4 - s]
        carry_ref[0:SUBLANES, c0:c0 + 512] = u[tm - SUBLANES:tm]
        upre_ref[:, c0:c0 + 512] = u[tm - SUBLANES:tm]
        return out

    def l2n(y):
        outs = []
        for hh in range(DN_HEADS):
            yh = y[:, hh * DN_DK:(hh + 1) * DN_DK]
            outs.append(yh * lax.rsqrt(jnp.sum(yh * yh, axis=-1, keepdims=True) + EPS))
        return jnp.concatenate(outs, axis=1)

    q_ref[...] = l2n(_silu(conv_section(_dot(h, w_ref[:, 0:512]), 0))) * (DN_DK ** -0.5)
    k_ref[...] = l2n(_silu(conv_section(_dot(h, w_ref[:, 512:1024]), 512)))
    v_ref[...] = _silu(conv_section(_dot(h, w_ref[:, 1024:1536]), 1024))
    z_ref[...] = _dot(h, w_ref[:, EV_Z:EV_Z + 512])
    qsw_ref[...] = _dot(h, w_ref[:, EV_QSW:EV_QSW + 512])
    ksw = _dot(h, w_ref[:, EV_KSW:EV_KSW + 128])
    vsw = _dot(h, w_ref[:, EV_VSW:EV_VSW + 128])
    ksw_ref[...] = ksw
    vsw_ref[...] = vsw
    tw = kvt_ref.shape[0]
    kvt_ref[:, 0:128] = ksw[tm - tw:tm]
    kvt_ref[:, 128:256] = vsw[tm - tw:tm]
    ab = _dot(h, w_ref[:, EV_AB:EV_AB + LANES])
    g = -jnp.exp(alog_ref[...]) * _softplus(ab + dtb_ref[...])
    lane = lax.broadcasted_iota(jnp.int32, ab.shape, 1)
    gb = jnp.where(lane < DN_HEADS, g, jax.nn.sigmoid(ab))
    gb_ref[...] = gb[:, 0:2 * DN_HEADS]
    ax = _dot(h, w_ref[:, EV_AX:EV_AX + CAT])
    gx_ref[...] = -jnp.exp(alogx_ref[...]) * _softplus(ax + dtbx_ref[...])
    bx_ref[...] = jax.nn.sigmoid(_dot(h, w_ref[:, EV_BX:EV_BX + CAT]))


def _even_in(x, nw, layer, w, cw, alog, dtb, alogx, dtbx, sub, hist, *, tm, seq_len, per_token_hist):
    n, d = x.shape
    nt = n // tm
    tiles_per_seq = max(seq_len // tm, 1)
    row = lambda c: pl.BlockSpec((tm, c), lambda i: (i, 0))
    cols = (512, 512, 512, 512, 2 * DN_HEADS, CAT, CAT, 512, 128, 128)
    out_shapes = [jax.ShapeDtypeStruct((n, c), F32) for c in cols]
    out_specs = [row(c) for c in cols]
    if per_token_hist:
        hist_spec = pl.BlockSpec((DN_CONV - 1, tm, EV_QKV), lambda i: (0, i, 0))
        out_shapes.append(jax.ShapeDtypeStruct((n, EV_QKV), F32))
        out_specs.append(row(EV_QKV))
    else:
        hist_spec = _const_spec(hist.shape)
        out_shapes.append(jax.ShapeDtypeStruct((nt, SUBLANES, EV_QKV), F32))
        out_specs.append(pl.BlockSpec((None, SUBLANES, EV_QKV), lambda i: (i, 0, 0)))
    tw = min(WINDOW, tm)
    out_shapes.append(jax.ShapeDtypeStruct((nt, tw, 2 * SW_KV_HEADS * SW_HD), F32))
    out_specs.append(pl.BlockSpec((None, tw, 2 * SW_KV_HEADS * SW_HD), lambda i: (i, 0, 0)))
    return pl.pallas_call(
        functools.partial(_even_in_kernel, tiles_per_seq=tiles_per_seq, per_token_hist=per_token_hist),
        grid=(nt,),
        in_specs=([row(d), _layer_spec(nw, layer)] + [_layer_spec(a, sub) for a in (w, cw, alog, dtb, alogx, dtbx)]
                  + [hist_spec]),
        out_specs=out_specs,
        out_shape=out_shapes,
        scratch_shapes=[pltpu.VMEM((SUBLANES + (0 if per_token_hist else tm), EV_QKV), F32)],
        compiler_params=_cparams(("arbitrary",)),
        name="even_in",
    )(x, nw, w, cw, alog, dtb, alogx, dtbx, hist)


OD_K = 512
OD_V = 1024
OD_R = 2048
OD_G = 3072
OD_COLS = OD_G + LANES


def _odd_in_kernel(x_ref, nw_ref, w_ref, wg_ref, bg_ref, q_ref, k_ref, v_ref, r_ref, gl_ref):
    h = _rms(x_ref[...], nw_ref[...]).astype(BF16)
    q_ref[...] = _dot(h, w_ref[:, 0:512]) * (GLA_DK ** -0.5)
    k_ref[...] = _dot(h, w_ref[:, OD_K:OD_K + 512])
    for c in range(2):
        v_ref[:, c * 512:(c + 1) * 512] = _dot(h, w_ref[:, OD_V + c * 512:OD_V + (c + 1) * 512])
        r_ref[:, c * 512:(c + 1) * 512] = _dot(h, w_ref[:, OD_R + c * 512:OD_R + (c + 1) * 512])
    gd = _dot(h, w_ref[:, OD_G:OD_G + LANES])
    x = _dot(gd.astype(BF16), wg_ref[...]) + bg_ref[...]
    gl_ref[...] = _log_sigmoid(x) * (1.0 / GLA_TAU)


def _odd_in(x, nw, layer, w, wg, bg, sub, *, tm):
    n, d = x.shape
    row = lambda c: pl.BlockSpec((tm, c), lambda i: (i, 0))
    cols = (512, 512, 1024, 1024, 512)
    return pl.pallas_call(
        _odd_in_kernel,
        grid=(n // tm,),
        in_specs=[row(d), _layer_spec(nw, layer)] + [_layer_spec(a, sub) for a in (w, wg, bg)],
        out_specs=[row(c) for c in cols],
        out_shape=[jax.ShapeDtypeStruct((n, c), F32) for c in cols],
        compiler_params=_cparams(("parallel",)),
        name="odd_in",
    )(x, nw, w, wg, bg)


FF_CHUNK = 512


def _post_mlp_kernel(*refs, even, final):
    if even:
        x_ref, odn_ref, z_ref, osw_ref, gn_ref, wo_ref, nm_ref, wu_ref, wd_ref = refs[:9]
        rest = refs[9:]
    else:
        x_ref, o_ref, r_ref, gn_ref, wo_ref, nm_ref, wu_ref, wd_ref = refs[:8]
        rest = refs[8:]
    if final:
        nf_ref, y_ref = rest
    else:
        (y_ref,) = rest

    x = x_ref[...]
    gn = gn_ref[...]
    if even:
        parts = []
        for hh in range(DN_HEADS):
            sl = slice(hh * DN_DV, (hh + 1) * DN_DV)
            parts.append(_rms(odn_ref[:, sl], gn) * _silu(z_ref[:, sl]))
        a = jnp.concatenate(parts, axis=1).astype(BF16)
        nv = DN_HEADS * DN_DV
        mix = _dot(a, wo_ref[0:nv, :]) + _dot(osw_ref[...].astype(BF16), wo_ref[nv:, :])
    else:
        parts = []
        for hh in range(GLA_HEADS):
            sl = slice(hh * GLA_DV, (hh + 1) * GLA_DV)
            parts.append(_rms(o_ref[:, sl], gn) * _silu(r_ref[:, sl]))
        mix = _dot(jnp.concatenate(parts, axis=1).astype(BF16), wo_ref[...])
    x1 = x + mix
    h2 = _rms(x1, nm_ref[...]).astype(BF16)
    d_ff = wu_ref.shape[1]
    acc = x1
    for c in range(d_ff // FF_CHUNK):
        sl = slice(c * FF_CHUNK, (c + 1) * FF_CHUNK)
        up = jnp.maximum(_dot(h2, wu_ref[:, sl]), 0.0)
        acc = acc + _dot((up * up).astype(BF16), wd_ref[sl, :])
    if final:
        acc = _rms(acc, nf_ref[...])
    y_ref[...] = acc


def _post_mlp(x, mixer_outs, gn, wo, nm, wu, wd, nf, layer, sub, *, tm, even):
    n, d = x.shape
    row = lambda c: pl.BlockSpec((tm, c), lambda i: (i, 0))
    final = nf is not None
    ins = [x, *mixer_outs, gn, wo, nm, wu, wd]
    specs = ([row(d)] + [row(a.shape[1]) for a in mixer_outs]
             + [_layer_spec(gn, sub), _layer_spec(wo, sub)] + [_layer_spec(a, layer) for a in (nm, wu, wd)])
    if final:
        ins.append(nf)
        specs.append(_const_spec(nf.shape))
    return pl.pallas_call(
        functools.partial(_post_mlp_kernel, even=even, final=final),
        grid=(n // tm,),
        in_specs=specs,
        out_specs=row(d),
        out_shape=jax.ShapeDtypeStruct((n, d), F32),
        compiler_params=_cparams(("parallel",)),
        name="post_mlp_even" if even else "post_mlp_odd",
    )(*ins)


def _block_diag(p16, row_blk, col_blk):
    R, Cc = p16.shape
    t = jnp.concatenate([p16] * DN_HEADS, axis=0)
    ri = lax.broadcasted_iota(jnp.int32, t.shape, 0) // row_blk
    ci = lax.broadcasted_iota(jnp.int32, t.shape, 1) // col_blk
    return jnp.where(ri == ci, t, jnp.zeros_like(t))


def _widen(x):
    lane = lax.broadcasted_iota(jnp.int32, (x.shape[0], LANES), 1)
    outs = []
    for t in range(CAT // LANES):
        xt = x[:, t * LANES:(t + 1) * LANES]
        xr = pltpu.roll(xt, CHUNK, 1)
        outs.append(jnp.where(lane < CHUNK, xt, xr))
        outs.append(jnp.where(lane < CHUNK, xr, xt))
    return jnp.concatenate(outs, axis=1)


def _dn_prompt_kernel(q_ref, k_ref, v_ref, gx_ref, bx_ref, grow_ref, brow_ref, o_ref, s_ref, st_ref, *, chunks):
    t = pl.program_id(0)
    nb = q_ref.shape[0]

    @pl.when(t == 0)
    def _():
        st_ref[...] = jnp.zeros_like(st_ref)

    C = CHUNK
    ri = lax.broadcasted_iota(jnp.int32, (C, CAT), 0)
    cj = lax.broadcasted_iota(jnp.int32, (C, CAT), 1) & (C - 1)
    tri_incl = ri >= cj
    tri_strict = ri > cj
    eye = (ri == cj).astype(F32)
    n_sq = max((C - 1).bit_length() - 1, 0)

    Gcols = [_seg_cumsum(gx_ref[b], C, 0) for b in range(nb)]
    Grows = [_seg_cumsum(grow_ref[b], C, 1) for b in range(nb)]
    ids = [(b, c) for c in range(chunks) for b in range(nb)]
    U = {}
    for (b, c) in ids:
        rs = slice(c * C, (c + 1) * C)
        Gc = Gcols[b][rs]
        Gr = Grows[b][c:c + 1]
        decay = jnp.where(tri_incl, jnp.exp(jnp.where(tri_incl, Gc - Gr, 0.0)), 0.0)
        k = k_ref[b, rs, :]
        q = q_ref[b, rs, :]
        k16 = k.astype(BF16)
        kq = _dot_nt(jnp.concatenate([k16, q.astype(BF16)], axis=0), _block_diag(k16, C, DN_DK))
        M = jnp.where(tri_strict, kq[0:C] * bx_ref[b, rs, :] * decay, 0.0)
        U[b, c] = dict(Gc=Gc, Gr=Gr, k=k, q=q, k16=k16, attn=kq[C:] * decay, X=eye - M, P16=M.astype(BF16))
    for (b, c) in ids:
        u_ = U[b, c]
        u_["P16"] = _dot(u_["P16"], _block_diag(u_["P16"], C, C)).astype(BF16)
    for s in range(n_sq):
        for (b, c) in ids:
            u_ = U[b, c]
            bd = _block_diag(u_["P16"], C, C)
            if s + 1 < n_sq:
                y = _dot(jnp.concatenate([u_["X"].astype(BF16), u_["P16"]], axis=0), bd)
                u_["X"] = u_["X"] + y[0:C]
                u_["P16"] = y[C:].astype(BF16)
            else:
                u_["X"] = u_["X"] + _dot(u_["X"].astype(BF16), bd)
    for (b, c) in ids:
        u_ = U[b, c]
        rs = slice(c * C, (c + 1) * C)
        Gc, Gr, k, q, k16, X = u_["Gc"], u_["Gr"], u_["k"], u_["q"], u_["k16"], u_["X"]
        brow = brow_ref[b, c:c + 1, :]
        Tu = (X * brow).astype(BF16)
        Tw = (X * (brow * jnp.exp(Gr))).astype(BF16)
        g_last = Gc[C - 1:C, :]
        khat = (k * _widen(jnp.exp(g_last - Gc))).astype(BF16)
        qe = (q * _widen(jnp.exp(Gc))).astype(BF16)
        v16 = v_ref[b, rs, :].astype(BF16)
        heads = []
        for hh in range(DN_HEADS):
            sl = slice(hh * DN_DK, (hh + 1) * DN_DK)
            cs = slice(hh * C, (hh + 1) * C)
            u = _dot(Tu[:, cs], v16[:, sl])
            w = _dot(Tw[:, cs], k16[:, sl])
            heads.append((u, jnp.concatenate([w.astype(BF16), qe[:, sl]], axis=0),
                          u_["attn"][:, cs].astype(BF16), khat[:, sl], jnp.exp(g_last[:, hh * C:hh * C + 1])))
        U[b, c] = heads

    state = [[st_ref[b, hh] for hh in range(DN_HEADS)] for b in range(nb)]
    streams = [(b, hh) for b in range(nb) for hh in range(DN_HEADS)]
    for c in range(chunks):
        rs = slice(c * C, (c + 1) * C)
        R, vn16 = {}, {}
        for (b, hh) in streams:
            R[b, hh] = _dot(U[b, c][hh][1], state[b][hh].astype(BF16))
        for (b, hh) in streams:
            vn16[b, hh] = (U[b, c][hh][0] - R[b, hh][0:C]).astype(BF16)
            o_ref[b, rs, hh * DN_DK:(hh + 1) * DN_DK] = R[b, hh][C:] + _dot(U[b, c][hh][2], vn16[b, hh])
        for (b, hh) in streams:
            state[b][hh] = state[b][hh] * U[b, c][hh][4] + _dot_tn(U[b, c][hh][3], vn16[b, hh])
    for b in range(nb):
        for hh in range(DN_HEADS):
            st_ref[b, hh] = state[b][hh]

    @pl.when(t == pl.num_programs(0) - 1)
    def _():
        s_ref[...] = st_ref[...]


def _dn_prompt(q, k, v, gb, gx, bx, *, batch, seq_len, tc):
    n = q.shape[0]
    nt = seq_len // tc
    chunks = tc // CHUNK
    hd = DN_HEADS * DN_DK
    gb5 = jnp.swapaxes(gb.reshape(batch, nt, chunks, CHUNK, 2 * DN_HEADS), 3, 4)
    grow = gb5[:, :, :, 0:DN_HEADS].reshape(batch, nt, chunks, CAT)
    brow = gb5[:, :, :, DN_HEADS:].reshape(batch, nt, chunks, CAT)
    r3 = lambda a: a.reshape(batch, seq_len, a.shape[-1])
    tok = lambda c: pl.BlockSpec((batch, tc, c), lambda t: (0, t, 0))
    rowspec = pl.BlockSpec((batch, None, chunks, CAT), lambda t: (0, t, 0, 0))
    o, s = pl.pallas_call(
        functools.partial(_dn_prompt_kernel, chunks=chunks),
        grid=(nt,),
        in_specs=[tok(hd), tok(hd), tok(hd), tok(CAT), tok(CAT), rowspec, rowspec],
        out_specs=[tok(hd), _const_spec((batch, DN_HEADS, DN_DK, DN_DV))],
        out_shape=[jax.ShapeDtypeStruct((batch, seq_len, hd), F32),
                   jax.ShapeDtypeStruct((batch, DN_HEADS, DN_DK, DN_DV), F32)],
        scratch_shapes=[pltpu.VMEM((batch, DN_HEADS, DN_DK, DN_DV), F32)],
        compiler_params=_cparams(("arbitrary",)),
        name="dn_prompt",
    )(r3(q), r3(k), r3(v), r3(gx), r3(bx), grow, brow)
    return o.reshape(n, hd), s


def _row_in_tile(x, r):
    R, Ln = x.shape
    x3 = x.reshape(R // SUBLANES, SUBLANES, Ln)
    return jnp.broadcast_to(x3[:, r:r + 1, :], x3.shape).reshape(R, Ln)


def _gla_intra(q, k, G2, sub_i, tile_lower):
    C = q.shape[0]
    nt = C // SUBLANES
    zrow = lambda n: jnp.zeros((n, GLA_DK), BF16)
    qs = (q * jnp.exp2(G2 - _row_in_tile(G2, 0))).astype(BF16)
    q16 = q.astype(BF16)
    q1, k1 = [], []
    for I in range(1, nt):
        lo = I * SUBLANES
        q1.append(jnp.concatenate([zrow(lo), qs[lo:lo + SUBLANES]] + ([zrow(C - lo - SUBLANES)] if C - lo - SUBLANES else []),
                                  axis=0))
        k1.append(jnp.concatenate([(k[0:lo] * jnp.exp2(G2[lo:lo + 1] - G2[0:lo])).astype(BF16), zrow(C - lo)], axis=0))
    a1 = _dot_nt(jnp.concatenate(q1, axis=1), jnp.concatenate(k1, axis=1))
    q2, k2 = [], []
    for I in range(SUBLANES):
        q2.append(jnp.where(sub_i == I, q16, jnp.zeros_like(q16)))
        k2.append((k * jnp.exp2(jnp.minimum(_row_in_tile(G2, I) - G2, 0.0))).astype(BF16))
    a2 = _dot_nt(jnp.concatenate(q2, axis=1), jnp.concatenate(k2, axis=1))
    return a1 + jnp.where(tile_lower, a2, 0.0)


def _gla_prompt_kernel(q_ref, k_ref, v_ref, gl_ref, o_ref, s_ref, st_ref, *, chunks):
    t = pl.program_id(0)
    nb = q_ref.shape[0]

    @pl.when(t == 0)
    def _():
        st_ref[...] = jnp.zeros_like(st_ref)

    C = CHUNK
    sub_i = lax.broadcasted_iota(jnp.int32, (C, GLA_DK), 0) & (SUBLANES - 1)
    ri = lax.broadcasted_iota(jnp.int32, (C, C), 0)
    ci = lax.broadcasted_iota(jnp.int32, (C, C), 1)
    tile_lower = ((ri // SUBLANES) == (ci // SUBLANES)) & (ci <= ri)

    Gs = [_seg_cumsum(gl_ref[b] * LOG2E, C, 0) for b in range(nb)]
    ids = [(b, c, hh) for c in range(chunks) for b in range(nb) for hh in range(GLA_HEADS)]
    U = {}
    for (b, c, hh) in ids:
        rs = slice(c * C, (c + 1) * C)
        ks = slice(hh * GLA_DK, (hh + 1) * GLA_DK)
        q = q_ref[b, rs, ks]
        k = k_ref[b, rs, ks]
        G = Gs[b][rs, ks]
        A = _gla_intra(q, k, G, sub_i, tile_lower)
        gl = G[C - 1:C, :]
        U[b, c, hh] = dict(A=A.astype(BF16), qe=(q * jnp.exp2(G)).astype(BF16),
                           khat=(k * jnp.exp2(gl - G)).astype(BF16), e_last=jnp.exp2(gl))
    for (b, c, hh) in ids:
        u_ = U[b, c, hh]
        v16 = v_ref[b, c * C:(c + 1) * C, hh * GLA_DV:(hh + 1) * GLA_DV].astype(BF16)
        u_["o"] = _dot(u_["A"], v16)
        u_["B"] = _dot_tn(v16, u_["khat"])
    for b in range(nb):
        for hh in range(GLA_HEADS):
            St = st_ref[b, hh]
            for c in range(chunks):
                U[b, c, hh]["St"] = St.astype(BF16)
                St = St * U[b, c, hh]["e_last"] + U[b, c, hh]["B"]
            st_ref[b, hh] = St
    for (b, c, hh) in ids:
        u_ = U[b, c, hh]
        o_ref[b, c * C:(c + 1) * C, hh * GLA_DV:(hh + 1) * GLA_DV] = u_["o"] + _dot_nt(u_["qe"], u_["St"])

    @pl.when(t == pl.num_programs(0) - 1)
    def _():
        s_ref[...] = st_ref[...]


def _gla_prompt(q, k, v, gl, *, batch, seq_len, tc):
    n = q.shape[0]
    nt = seq_len // tc
    chunks = tc // CHUNK
    hk = GLA_HEADS * GLA_DK
    hv = GLA_HEADS * GLA_DV
    r3 = lambda a: a.reshape(batch, seq_len, a.shape[-1])
    tok = lambda c: pl.BlockSpec((batch, tc, c), lambda t: (0, t, 0))
    o, s = pl.pallas_call(
        functools.partial(_gla_prompt_kernel, chunks=chunks),
        grid=(nt,),
        in_specs=[tok(hk), tok(hk), tok(hv), tok(hk)],
        out_specs=[tok(hv), _const_spec((batch, GLA_HEADS, GLA_DV, GLA_DK))],
        out_shape=[jax.ShapeDtypeStruct((batch, seq_len, hv), F32),
                   jax.ShapeDtypeStruct((batch, GLA_HEADS, GLA_DV, GLA_DK), F32)],
        scratch_shapes=[pltpu.VMEM((batch, GLA_HEADS, GLA_DV, GLA_DK), F32)],
        compiler_params=_cparams(("arbitrary",)),
        name="gla_prompt",
    )(r3(q), r3(k), r3(v), r3(gl))
    return o.reshape(n, hv), s


def _alibi_slope(h):
    return 2.0 ** (-8.0 * (h + 1) / SW_HEADS)


def _swa_prompt_kernel(sink_ref, q_ref, kc_ref, vc_ref, kp_ref, vp_ref, o_ref, *, sub):
    n = pl.program_id(1)
    W = WINDOW
    GW = SW_GROUP * W
    ri = lax.broadcasted_iota(jnp.int32, (GW, 2 * W), 0)
    ci = lax.broadcasted_iota(jnp.int32, (GW, 2 * W), 1)
    rel = (ri & (W - 1)) + W - ci
    valid = (rel >= 0) & (rel < W) & ((ci >= W) | (n > 0))
    relf = rel.astype(F32)
    gcol = lax.broadcasted_iota(jnp.int32, (GW, 1), 0) // W
    ones = jnp.ones((2 * W, SW_HD), BF16)
    scores, vext, sinks = [], [], []
    for kh in range(SW_KV_HEADS):
        sl = slice(kh * SW_HD, (kh + 1) * SW_HD)
        K = jnp.concatenate([kp_ref[:, sl], kc_ref[:, sl]], axis=0).astype(BF16)
        V = jnp.concatenate([vp_ref[:, sl], vc_ref[:, sl]], axis=0).astype(BF16)
        vext.append(jnp.concatenate([V, ones], axis=1))
        qg = jnp.concatenate([q_ref[:, (kh * SW_GROUP + g) * SW_HD:(kh * SW_GROUP + g + 1) * SW_HD]
                              for g in range(SW_GROUP)], axis=0).astype(BF16)
        scores.append(_dot_nt(qg, K))
        slope = jnp.zeros((GW, 1), F32)
        sink = jnp.zeros((GW, 1), F32)
        for g in range(SW_GROUP):
            slope = jnp.where(gcol == g, _alibi_slope(kh * SW_GROUP + g), slope)
            sink = jnp.where(gcol == g, sink_ref[sub, kh * SW_GROUP + g], sink)
        sinks.append((slope, sink))
    for kh in range(SW_KV_HEADS):
        slope, sink = sinks[kh]
        s = jnp.where(valid, scores[kh] * (SW_HD ** -0.5) - slope * relf, -jnp.inf)
        m = jnp.maximum(jnp.max(s, axis=-1, keepdims=True), sink)
        pv = _dot(jnp.exp(s - m).astype(BF16), vext[kh])
        og = pv[:, 0:SW_HD] / (pv[:, SW_HD:SW_HD + 1] + jnp.exp(sink - m))
        for g in range(SW_GROUP):
            h = kh * SW_GROUP + g
            o_ref[:, h * SW_HD:(h + 1) * SW_HD] = og[g * W:(g + 1) * W]


def _swa_prompt(q, k, v, sinks, sub, *, batch, seq_len):
    n = q.shape[0]
    nb = seq_len // WINDOW
    cur = lambda c: pl.BlockSpec((WINDOW, c), lambda b, t: (b * nb + t, 0))
    prev = lambda c: pl.BlockSpec((WINDOW, c), lambda b, t: (b * nb + jnp.maximum(t - 1, 0), 0))
    kvc = SW_KV_HEADS * SW_HD
    return pl.pallas_call(
        functools.partial(_swa_prompt_kernel, sub=sub),
        grid=(batch, nb),
        in_specs=[pl.BlockSpec(memory_space=pltpu.SMEM), cur(SW_HEADS * SW_HD), cur(kvc), cur(kvc), prev(kvc), prev(kvc)],
        out_specs=cur(SW_HEADS * SW_HD),
        out_shape=jax.ShapeDtypeStruct((n, SW_HEADS * SW_HD), F32),
        compiler_params=_cparams(("parallel", "parallel")),
        name="swa_prompt",
    )(sinks, q, k, v, k, v)


SEQ_BLK = 8


def _state_call(kernel_fn, name, sub, stacked_states, prev_outs, other_ins, other_specs, row_out_shape, row_out_spec,
                smem_ins=()):
    nstate = len(stacked_states)

    def sspec(a):
        nd = a.ndim - 2
        return pl.BlockSpec((None, SEQ_BLK) + a.shape[2:], lambda i: (sub, i) + (0,) * nd)

    ns = stacked_states[0].shape[1]
    ins = list(smem_ins) + list(stacked_states) + list(other_ins)
    specs = ([pl.BlockSpec(memory_space=pltpu.SMEM)] * len(smem_ins) + [sspec(a) for a in stacked_states]
             + list(other_specs))
    aliases = {}
    if prev_outs is not None:
        for j, p in enumerate(prev_outs):
            aliases[len(ins)] = 1 + j
            ins.append(p)
            specs.append(pl.BlockSpec(memory_space=pl.ANY))

    def body(*refs):
        n_in = len(smem_ins) + nstate + len(other_ins)
        kernel_fn(*refs[:n_in], *refs[len(ins):])

    return pl.pallas_call(
        body,
        grid=(ns // SEQ_BLK,),
        in_specs=specs,
        out_specs=[row_out_spec] + [sspec(a) for a in stacked_states],
        out_shape=[row_out_shape] + [jax.ShapeDtypeStruct(a.shape, F32) for a in stacked_states],
        input_output_aliases=aliases,
        compiler_params=_cparams(("parallel",)),
        name=name,
    )(*ins)


def _cols_by_block(a):
    n, c = a.shape
    return jnp.swapaxes(a.reshape(n // SEQ_BLK, SEQ_BLK, c), 1, 2)


def _dn_decode_kernel(gb_ref, s_ref, qT_ref, kT_ref, v_ref, o_ref, sn_ref):
    i = pl.program_id(0)
    for j in range(SEQ_BLK):
        for hh in range(DN_HEADS):
            sl = slice(hh * DN_DK, (hh + 1) * DN_DK)
            S = s_ref[j, hh]
            kcol = kT_ref[0, sl, j:j + 1]
            qcol = qT_ref[0, sl, j:j + 1]
            vrow = v_ref[j:j + 1, sl]
            a = jnp.exp(jnp.full((1, DN_DV), gb_ref[i * SEQ_BLK + j, hh], F32))
            beta = gb_ref[i * SEQ_BLK + j, DN_HEADS + hh]
            Sa = S * a
            pred = jnp.sum(Sa * kcol, axis=0, keepdims=True)
            Sn = Sa + kcol * (beta * (vrow - pred))
            sn_ref[j, hh] = Sn
            o_ref[j:j + 1, sl] = jnp.sum(Sn * qcol, axis=0, keepdims=True)


def _dn_decode(S_all, prev, sub, q, k, v, gb):
    n = q.shape[0]
    hd = DN_HEADS * DN_DK
    colspec = pl.BlockSpec((1, hd, SEQ_BLK), lambda i: (i, 0, 0))
    rspec = pl.BlockSpec((SEQ_BLK, hd), lambda i: (i, 0))
    return _state_call(_dn_decode_kernel, "dn_decode", sub, [S_all], prev,
                       [_cols_by_block(q), _cols_by_block(k), v], [colspec, colspec, rspec],
                       jax.ShapeDtypeStruct((n, hd), F32), rspec, smem_ins=[gb])


def _gla_decode_kernel(s_ref, qT_ref, kT_ref, glT_ref, v_ref, o_ref, sn_ref):
    for j in range(SEQ_BLK):
        for hh in range(GLA_HEADS):
            ks = slice(hh * GLA_DK, (hh + 1) * GLA_DK)
            vs = slice(hh * GLA_DV, (hh + 1) * GLA_DV)
            S = s_ref[j, hh]
            kcol = kT_ref[0, ks, j:j + 1]
            qcol = qT_ref[0, ks, j:j + 1]
            acol = jnp.exp(glT_ref[0, ks, j:j + 1])
            vrow = v_ref[j:j + 1, vs]
            Sn = S * acol + kcol * vrow
            sn_ref[j, hh] = Sn
            o_ref[j:j + 1, vs] = jnp.sum(Sn * qcol, axis=0, keepdims=True)


def _gla_decode(S_all, prev, sub, q, k, v, gl):
    n = q.shape[0]
    hk = GLA_HEADS * GLA_DK
    hv = GLA_HEADS * GLA_DV
    colspec = pl.BlockSpec((1, hk, SEQ_BLK), lambda i: (i, 0, 0))
    rspec = pl.BlockSpec((SEQ_BLK, hv), lambda i: (i, 0))
    return _state_call(_gla_decode_kernel, "gla_decode", sub, [S_all], prev,
                       [_cols_by_block(q), _cols_by_block(k), _cols_by_block(gl), v],
                       [colspec, colspec, colspec, rspec],
                       jax.ShapeDtypeStruct((n, hv), F32), rspec)


def _swa_decode_kernel(sink_ref, kc_ref, vc_ref, q_ref, kn_ref, vn_ref, knT_ref, vnT_ref, o_ref, ko_ref, vo_ref, *, sub):
    W = WINDOW
    wi = lax.broadcasted_iota(jnp.int32, (SW_GROUP, W), 1)
    relf = (W - wi).astype(F32)
    valid = wi >= 1
    gi = lax.broadcasted_iota(jnp.int32, (SW_GROUP, 1), 0)
    last = lax.broadcasted_iota(jnp.int32, (SW_HD, W), 1) == W - 1

    def shifted(cache, new_col):
        return jnp.where(last, new_col, pltpu.roll(cache, W - 1, 1))

    heads = []
    for kh in range(SW_KV_HEADS):
        slope = jnp.zeros((SW_GROUP, 1), F32)
        sink = jnp.zeros((SW_GROUP, 1), F32)
        for g in range(SW_GROUP):
            slope = jnp.where(gi == g, _alibi_slope(kh * SW_GROUP + g), slope)
            sink = jnp.where(gi == g, sink_ref[sub, kh * SW_GROUP + g], sink)
        heads.append((slope, sink))
    ids = [(j, kh) for j in range(SEQ_BLK) for kh in range(SW_KV_HEADS)]
    U = {}
    for (j, kh) in ids:
        sl = slice(kh * SW_HD, (kh + 1) * SW_HD)
        qg = jnp.concatenate(
            [q_ref[j:j + 1, (kh * SW_GROUP + g) * SW_HD:(kh * SW_GROUP + g + 1) * SW_HD] for g in range(SW_GROUP)],
            axis=0)
        kt = kc_ref[j, kh]
        s = _dot(qg.astype(BF16), kt.astype(BF16))
        ko_ref[j, kh] = shifted(kt, knT_ref[0, sl, j:j + 1])
        U[j, kh] = (s, jnp.sum(qg * kn_ref[j:j + 1, sl], axis=-1, keepdims=True) * (SW_HD ** -0.5))
    for (j, kh) in ids:
        slope, sink = heads[kh]
        s, s_new = U[j, kh]
        s = jnp.where(valid, s * (SW_HD ** -0.5) - slope * relf, -jnp.inf)
        m = jnp.maximum(jnp.maximum(jnp.max(s, axis=-1, keepdims=True), s_new), sink)
        p = jnp.exp(s - m)
        p_new = jnp.exp(s_new - m)
        den = jnp.sum(p, axis=-1, keepdims=True) + p_new + jnp.exp(sink - m)
        U[j, kh] = ((p / den).astype(BF16), p_new / den)
    for (j, kh) in ids:
        sl = slice(kh * SW_HD, (kh + 1) * SW_HD)
        p16, w_new = U[j, kh]
        vt = vc_ref[j, kh]
        og = _dot_nt(p16, vt.astype(BF16)) + w_new * vn_ref[j:j + 1, sl]
        vo_ref[j, kh] = shifted(vt, vnT_ref[0, sl, j:j + 1])
        for g in range(SW_GROUP):
            h = kh * SW_GROUP + g
            o_ref[j:j + 1, h * SW_HD:(h + 1) * SW_HD] = og[g:g + 1]


def _swa_decode(kc_all, vc_all, prev, sub, q, kn, vn, sinks):
    n = q.shape[0]
    kvc = SW_KV_HEADS * SW_HD
    rq = pl.BlockSpec((SEQ_BLK, SW_HEADS * SW_HD), lambda i: (i, 0))
    rkv = pl.BlockSpec((SEQ_BLK, kvc), lambda i: (i, 0))
    ckv = pl.BlockSpec((1, kvc, SEQ_BLK), lambda i: (i, 0, 0))
    return _state_call(functools.partial(_swa_decode_kernel, sub=sub), "swa_decode", sub, [kc_all, vc_all], prev,
                       [q, kn, vn, _cols_by_block(kn), _cols_by_block(vn)], [rq, rkv, rkv, ckv, ckv],
                       jax.ShapeDtypeStruct((n, SW_HEADS * SW_HD), F32), rq, smem_ins=[sinks])


def _pad_last(a, width):
    return jnp.pad(a, [(0, 0)] * (a.ndim - 1) + [(0, width - a.shape[-1])])


def kernel(x_prompt, x_sample, state_dn, state_dn_conv, cache_swa_k, cache_swa_v, state_gla,
           norm_mix, norm_mlp, norm_final,
           even_w_in, dn_conv_w, dn_a_log, dn_dt_bias, dn_norm, sw_sinks, even_w_out,
           gla_w_in, gla_w_gate_up, gla_b_gate, gla_norm, gla_w_out,
           mlp_w_up, mlp_w_down):
    B, L, D = x_prompt.shape
    NS = x_sample.shape[0]
    depth = norm_mix.shape[0]
    n_even = even_w_in.shape[0]
    n_odd = gla_w_in.shape[0]
    kvc = SW_KV_HEADS * SW_HD
    xp = x_prompt.reshape(B * L, D)
    xs = x_sample.reshape(NS, D)
    tm_p = 512
    tc_dn = 256
    tc_gla = 256
    no_hist = jnp.zeros((SUBLANES, LANES), F32)

    nmix = norm_mix.reshape(depth, 1, D)
    nmlp = norm_mlp.reshape(depth, 1, D)
    nfin = norm_final.reshape(1, D)
    wu = mlp_w_up.astype(BF16)
    wd = mlp_w_down.astype(BF16)
    ab0 = EV_QKV + 512
    sw0 = ab0 + 2 * DN_HEADS
    w_ab = even_w_in[:, :, ab0:sw0]
    ev_w = jnp.concatenate(
        [even_w_in[:, :, :ab0], even_w_in[:, :, sw0:], _pad_last(w_ab, LANES),
         jnp.repeat(w_ab[:, :, 0:DN_HEADS], CHUNK, axis=2), jnp.repeat(w_ab[:, :, DN_HEADS:], CHUNK, axis=2)],
        axis=2).astype(BF16)
    alog = _pad_last(dn_a_log, LANES).reshape(n_even, 1, LANES)
    dtb = _pad_last(dn_dt_bias, LANES).reshape(n_even, 1, LANES)
    alogx = jnp.repeat(dn_a_log, CHUNK, axis=1).reshape(n_even, 1, CAT)
    dtbx = jnp.repeat(dn_dt_bias, CHUNK, axis=1).reshape(n_even, 1, CAT)
    dn_gn = dn_norm.reshape(n_even, 1, DN_DV)
    ev_wo = even_w_out.astype(BF16)
    sinks = sw_sinks.astype(F32)
    od_w = _pad_last(gla_w_in, OD_COLS).astype(BF16)
    od_wg = jnp.pad(gla_w_gate_up, ((0, 0), (0, LANES - GLA_RANK), (0, 0))).astype(BF16)
    od_bg = gla_b_gate.reshape(n_odd, 1, -1)
    gla_gn = gla_norm.reshape(n_odd, 1, GLA_DV)
    od_wo = gla_w_out.astype(BF16)
    kc_all = jnp.transpose(cache_swa_k, (0, 1, 3, 4, 2))
    vc_all = jnp.transpose(cache_swa_v, (0, 1, 3, 4, 2))

    p_dn, p_conv, p_k, p_v, p_gla, s_conv = [], [], [], [], [], []
    s_dn = s_kv = s_gla = None
    for layer in range(depth):
        nf = nfin if layer == depth - 1 else None
        if layer % 2 == 0:
            e = layer // 2
            q, k, v, z, gb, gx, bx, qsw, ksw, vsw, tail, kvt = _even_in(
                xp, nmix, layer, ev_w, dn_conv_w, alog, dtb, alogx, dtbx, e, no_hist,
                tm=tm_p, seq_len=L, per_token_hist=False)
            o_dn, S = _dn_prompt(q, k, v, gb, gx, bx, batch=B, seq_len=L, tc=tc_dn)
            o_sw = _swa_prompt(qsw, ksw, vsw, sinks, e, batch=B, seq_len=L)
            xp = _post_mlp(xp, (o_dn, z, o_sw), dn_gn, ev_wo, nmlp, wu, wd, nf, layer, e, tm=tm_p, even=True)
            p_dn.append(S)
            tiles = L // tm_p
            p_conv.append(tail.reshape(B, tiles, SUBLANES, EV_QKV)[:, tiles - 1, SUBLANES - (DN_CONV - 1):])
            kv_end = kvt.reshape(B, tiles, WINDOW, 2 * kvc)[:, tiles - 1]
            p_k.append(kv_end[:, :, 0:kvc].reshape(B, WINDOW, SW_KV_HEADS, SW_HD))
            p_v.append(kv_end[:, :, kvc:].reshape(B, WINDOW, SW_KV_HEADS, SW_HD))
            hist = jnp.swapaxes(state_dn_conv[e], 0, 1)
            q, k, v, z, gb, gx, bx, qsw, ksw, vsw, upre, _ = _even_in(
                xs, nmix, layer, ev_w, dn_conv_w, alog, dtb, alogx, dtbx, e, hist,
                tm=NS, seq_len=1, per_token_hist=True)
            o_dn, s_dn_new = _dn_decode(state_dn, None if s_dn is None else [s_dn], e, q, k, v, gb)
            s_dn = s_dn_new
            o_sw, nk, nv = _swa_decode(kc_all, vc_all, s_kv, e, qsw, ksw, vsw, sinks)
            s_kv = [nk, nv]
            xs = _post_mlp(xs, (o_dn, z, o_sw), dn_gn, ev_wo, nmlp, wu, wd, nf, layer, e, tm=NS, even=True)
            s_conv.append(jnp.concatenate([state_dn_conv[e][:, 1:], upre[:, None, :]], axis=1))
        else:
            o_i = layer // 2
            q, k, v, r, gl = _odd_in(xp, nmix, layer, od_w, od_wg, od_bg, o_i, tm=tm_p)
            o, St = _gla_prompt(q, k, v, gl, batch=B, seq_len=L, tc=tc_gla)
            xp = _post_mlp(xp, (o, r), gla_gn, od_wo, nmlp, wu, wd, nf, layer, o_i, tm=tm_p, even=False)
            p_gla.append(jnp.swapaxes(St, 2, 3))
            q, k, v, r, gl = _odd_in(xs, nmix, layer, od_w, od_wg, od_bg, o_i, tm=NS)
            o, s_gla_new = _gla_decode(state_gla, None if s_gla is None else [s_gla], o_i, q, k, v, gl)
            s_gla = s_gla_new
            xs = _post_mlp(xs, (o, r), gla_gn, od_wo, nmlp, wu, wd, nf, layer, o_i, tm=NS, even=False)

    to_cache = lambda a: jnp.transpose(a, (0, 1, 4, 2, 3))
    return (xp.reshape(B, L, D), xs.reshape(NS, 1, D),
            jnp.stack(p_dn), jnp.stack(p_conv), jnp.stack(p_k), jnp.stack(p_v), jnp.stack(p_gla),
            s_dn, jnp.stack(s_conv), to_cache(s_kv[0]), to_cache(s_kv[1]), s_gla)
```
